```python
import math
import jax, jax.numpy as jnp
from jax import lax
import numpy as np

D_MODEL = 1024
BATCH = 4
SEQ = 8192
DEPTH = 1

GRID_W = 64
CTX_LEN = 256
EPS = 1e-6
ROPE_THETA = 10000.0
Q_BLOCK = 128

DIFF_HEADS = 4
DIFF_HEAD_DIM = 64
DIFF_V_DIM = 2 * DIFF_HEAD_DIM
DIFF_WIDTH = DIFF_HEADS * DIFF_V_DIM
DIFF_SCALE = DIFF_HEAD_DIM ** -0.5

MLA_HEADS = 4
MLA_NOPE = 128
MLA_ROPE = 64
MLA_V = 128
MLA_Q_RANK = 256
MLA_KV_RANK = 128
MLA_WIDTH = MLA_HEADS * MLA_V
MLA_SCALE = (MLA_NOPE + MLA_ROPE) ** -0.5

MIX_WIDTH = DIFF_WIDTH + MLA_WIDTH
ROPE_PAIRS = DIFF_HEAD_DIM // 4

IN_SPLITS = (DIFF_HEADS * 2 * DIFF_HEAD_DIM,
             DIFF_HEADS * 2 * DIFF_HEAD_DIM,
             DIFF_HEADS * DIFF_V_DIM,
             MLA_Q_RANK,
             MLA_KV_RANK,
             MLA_ROPE)
IN_COLS = sum(IN_SPLITS)

N_EXPERTS = 32
TOP_K = 4
D_FF = 1024
SWIGLU_LIMIT = 7.0
SWIGLU_ALPHA = 1.702
MOE_BLOCK = 128

kernel_name = "hybrid_diffattn_mla_moe_dit_layer"


def rmsnorm(x, w):
    xf = x.astype(jnp.float32)
    y = xf * lax.rsqrt(jnp.mean(xf * xf, axis=-1, keepdims=True) + EPS)
    return (y * w.astype(jnp.float32)).astype(x.dtype)


def modulate(h, shift, scale):
    return h * (1 + scale) + shift


def axial_angles(n):
    rows = n // GRID_W
    row = jnp.repeat(jnp.arange(rows, dtype=jnp.float32), GRID_W)
    col = jnp.tile(jnp.arange(GRID_W, dtype=jnp.float32), rows)
    inv = ROPE_THETA ** (-jnp.arange(ROPE_PAIRS, dtype=jnp.float32) / ROPE_PAIRS)
    return jnp.stack([row[:, None] * inv, col[:, None] * inv], axis=1)


def apply_axial_rope(x, ang):
    p = x.shape[-1] // 4
    xf = x.astype(jnp.float32).reshape(*x.shape[:-1], 2, 2, p)
    x1, x2 = xf[..., 0, :], xf[..., 1, :]
    cos, sin = jnp.cos(ang), jnp.sin(ang)
    out = jnp.stack([x1 * cos - x2 * sin, x2 * cos + x1 * sin], axis=-2)
    return out.reshape(x.shape).astype(x.dtype)


def mixer_inputs(h, w_in, q_norm_w, kv_norm_w, w_uq, w_ukv, ang):
    B, n, _ = h.shape
    proj = h @ w_in
    cuts = list(np.cumsum(IN_SPLITS)[:-1])
    dq, dk, dv, cq, ckv, kr = jnp.split(proj, cuts, axis=-1)
    dq = dq.reshape(B, n, DIFF_HEADS, 2, DIFF_HEAD_DIM)
    dk = dk.reshape(B, n, DIFF_HEADS, 2, DIFF_HEAD_DIM)
    dv = dv.reshape(B, n, DIFF_HEADS, DIFF_V_DIM)
    q = (rmsnorm(cq, q_norm_w) @ w_uq).reshape(B, n, MLA_HEADS, MLA_NOPE + MLA_ROPE)
    kv = (rmsnorm(ckv, kv_norm_w) @ w_ukv).reshape(B, n, MLA_HEADS, MLA_NOPE + MLA_V)
    q_nope, q_rope = q[..., :MLA_NOPE], q[..., MLA_NOPE:]
    k_nope, mv = kv[..., :MLA_NOPE], kv[..., MLA_NOPE:]
    if ang is not None:
        dq = apply_axial_rope(dq, ang[:, None, None])
        dk = apply_axial_rope(dk, ang[:, None, None])
        q_rope = apply_axial_rope(q_rope, ang[:, None])
        kr = apply_axial_rope(kr, ang)
    return (dq, dk, dv, q_nope, q_rope, k_nope, kr, mv)


def diff_attend(q, k, v, lam):
    s = jnp.einsum('bqhcd,bkhcd->bchqk', q, k).astype(jnp.float32) * DIFF_SCALE
    p = jax.nn.softmax(s, axis=-1)
    p = p[:, 0] - lam * p[:, 1]
    return jnp.einsum('bhqk,bkhe->bqhe', p.astype(v.dtype), v)


def mla_attend(q_nope, q_rope, k_nope, k_rope, v):
    s = (jnp.einsum('bqhd,bkhd->bhqk', q_nope, k_nope)
         + jnp.einsum('bqhr,bkr->bhqk', q_rope, k_rope)).astype(jnp.float32) * MLA_SCALE
    p = jax.nn.softmax(s, axis=-1)
    return jnp.einsum('bhqk,bkhe->bqhe', p.astype(v.dtype), v)


def to_blocks(t):
    B, n = t.shape[:2]
    return jnp.moveaxis(t.reshape(B, n // Q_BLOCK, Q_BLOCK, *t.shape[2:]), 1, 0)


def from_blocks(t):
    t = jnp.moveaxis(t, 0, 1)
    return t.reshape(t.shape[0], -1, *t.shape[3:])


def merge_heads(o_diff, o_mla, subln_w, lam_init, w_out):
    B, n = o_diff.shape[:2]
    o_diff = rmsnorm(o_diff, subln_w) * (1.0 - lam_init)
    mix = jnp.concatenate([o_diff.reshape(B, n, DIFF_WIDTH), o_mla.reshape(B, n, MLA_WIDTH)], axis=-1)
    return mix @ w_out


def clamped_swiglu(g, u):
    g = jnp.minimum(g, SWIGLU_LIMIT)
    u = jnp.clip(u, -SWIGLU_LIMIT, SWIGLU_LIMIT)
    return g * jax.nn.sigmoid(SWIGLU_ALPHA * g) * (u + 1)


def moe(h, router_w, router_b, w1, b1, w2, b2):
    T, D = h.shape
    logits = (h @ router_w + router_b).astype(jnp.float32)
    top_v, top_i = lax.top_k(logits, TOP_K)
    gates = jax.nn.softmax(top_v, axis=-1).astype(h.dtype)
    flat_e = top_i.reshape(-1)
    flat_tok = jnp.repeat(jnp.arange(T, dtype=jnp.int32), TOP_K)
    order = jnp.argsort(flat_e)
    e_sorted = flat_e[order]
    tok_sorted = flat_tok[order]
    gate_sorted = gates.reshape(-1)[order]
    counts = jnp.bincount(flat_e, length=N_EXPERTS)
    padded = ((counts + MOE_BLOCK - 1) // MOE_BLOCK) * MOE_BLOCK
    ends = jnp.cumsum(padded)
    pad_start = ends - padded
    grp_start = jnp.cumsum(counts) - counts
    n_assign = T * TOP_K
    rank = jnp.arange(n_assign, dtype=jnp.int32) - grp_start[e_sorted]
    dest = pad_start[e_sorted] + rank
    n_blocks = (n_assign + N_EXPERTS * (MOE_BLOCK - 1) + MOE_BLOCK - 1) // MOE_BLOCK
    row_tok = jnp.full((n_blocks * MOE_BLOCK,), T, jnp.int32).at[dest].set(tok_sorted)
    h_pad = jnp.concatenate([h, jnp.zeros((1, D), h.dtype)], axis=0)
    xs = h_pad[row_tok].reshape(n_blocks, MOE_BLOCK, D)
    blk_e = jnp.minimum(jnp.searchsorted(ends, jnp.arange(n_blocks) * MOE_BLOCK, side='right'),
                        N_EXPERTS - 1)

    def expert_block(args):
        xb, e = args
        hid = xb @ w1[e] + b1[e]
        return clamped_swiglu(hid[:, :D_FF], hid[:, D_FF:]) @ w2[e] + b2[e]

    ys = lax.map(expert_block, (xs, blk_e)).reshape(-1, D)
    contrib = ys[dest] * gate_sorted[:, None]
    return jax.ops.segment_sum(contrib, tok_sorted, num_segments=T)


def setup_inputs(seed: int = 0) -> dict:
    key = jax.random.key(seed)
    ks = jax.random.split(key, 32)
    f32 = jnp.float32
    L = DEPTH

    def nrm(k, shape, scale):
        return jax.random.normal(k, shape, f32) * scale

    def gain(k, shape):
        return 1.0 + 0.05 * jax.random.normal(k, shape, f32)

    return {
        'x': nrm(ks[0], (BATCH, SEQ, D_MODEL), 1.0),
        'c': nrm(ks[1], (BATCH, D_MODEL), 1.0),
        'ctx': nrm(ks[2], (BATCH, CTX_LEN, D_MODEL), 1.0),
        'c_ctx': nrm(ks[3], (D_MODEL,), 1.0),
        'w_ada': nrm(ks[4], (L, D_MODEL, 6 * D_MODEL), 0.5 * D_MODEL ** -0.5),
        'b_ada': nrm(ks[5], (L, 6 * D_MODEL), 0.02),
        'attn_norm_w': gain(ks[6], (L, D_MODEL)),
        'w_in': nrm(ks[7], (L, D_MODEL, IN_COLS), D_MODEL ** -0.5),
        'q_norm_w': gain(ks[8], (L, MLA_Q_RANK)),
        'kv_norm_w': gain(ks[9], (L, MLA_KV_RANK)),
        'w_uq': nrm(ks[10], (L, MLA_Q_RANK, MLA_HEADS * (MLA_NOPE + MLA_ROPE)), MLA_Q_RANK ** -0.5),
        'w_ukv': nrm(ks[11], (L, MLA_KV_RANK, MLA_HEADS * (MLA_NOPE + MLA_V)), MLA_KV_RANK ** -0.5),
        'lambda_q1': nrm(ks[12], (L, DIFF_HEAD_DIM), 0.1),
        'lambda_k1': nrm(ks[13], (L, DIFF_HEAD_DIM), 0.1),
        'lambda_q2': nrm(ks[14], (L, DIFF_HEAD_DIM), 0.1),
        'lambda_k2': nrm(ks[15], (L, DIFF_HEAD_DIM), 0.1),
        'subln_w': gain(ks[16], (L, DIFF_V_DIM)),
        'w_out': nrm(ks[17], (L, MIX_WIDTH, D_MODEL), MIX_WIDTH ** -0.5),
        'ffn_norm_w': gain(ks[18], (L, D_MODEL)),
        'router_w': nrm(ks[19], (L, D_MODEL, N_EXPERTS), D_MODEL ** -0.5),
        'router_b': nrm(ks[20], (L, N_EXPERTS), 0.01),
        'w1': nrm(ks[21], (L, N_EXPERTS, D_MODEL, 2 * D_FF), D_MODEL ** -0.5),
        'b1': nrm(ks[22], (L, N_EXPERTS, 2 * D_FF), 0.02),
        'w2': nrm(ks[23], (L, N_EXPERTS, D_FF, D_MODEL), D_FF ** -0.5),
        'b2': nrm(ks[24], (L, N_EXPERTS, D_MODEL), 0.02),
        'final_norm_w': gain(ks[25], (D_MODEL,)),
    }


def reference(x, c, ctx, c_ctx, w_ada, b_ada, attn_norm_w, w_in, q_norm_w, kv_norm_w, w_uq, w_ukv,
              lambda_q1, lambda_k1, lambda_q2, lambda_k2, subln_w, w_out, ffn_norm_w,
              router_w, router_b, w1, b1, w2, b2, final_norm_w):
    B, n, D = x.shape
    ang = axial_angles(n)
    for l in range(DEPTH):
        last = l == DEPTH - 1
        lam_init = 0.8 - 0.6 * math.exp(-0.3 * l)
        mod_lat = (jax.nn.silu(c) @ w_ada[l] + b_ada[l])[:, None, :]
        mod_ctx = (jax.nn.silu(c_ctx) @ w_ada[l] + b_ada[l])[None, None, :]
        sa, sca, ga, sf, scf, gf = jnp.split(mod_lat, 6, axis=-1)
        csa, csca, cga, csf, cscf, cgf = jnp.split(mod_ctx, 6, axis=-1)
        lam = (jnp.exp(jnp.sum(lambda_q1[l].astype(jnp.float32) * lambda_k1[l].astype(jnp.float32)))
               - jnp.exp(jnp.sum(lambda_q2[l].astype(jnp.float32) * lambda_k2[l].astype(jnp.float32)))
               + lam_init)
        mp = (w_in[l], q_norm_w[l], kv_norm_w[l], w_uq[l], w_ukv[l])

        l_dq, l_dk, l_dv, l_qn, l_qr, l_kn, l_kr, l_v = mixer_inputs(
            modulate(rmsnorm(x, attn_norm_w[l]), sa, sca), *mp, ang)
        c_dq, c_dk, c_dv, c_qn, c_qr, c_kn, c_kr, c_v = mixer_inputs(
            modulate(rmsnorm(ctx, attn_norm_w[l]), csa, csca), *mp, None)
        dk_all = jnp.concatenate([c_dk, l_dk], axis=1)
        dv_all = jnp.concatenate([c_dv, l_dv], axis=1)
        kn_all = jnp.concatenate([c_kn, l_kn], axis=1)
        kr_all = jnp.concatenate([c_kr, l_kr], axis=1)
        v_all = jnp.concatenate([c_v, l_v], axis=1)
        o_diff = from_blocks(lax.map(lambda qb: diff_attend(qb, dk_all, dv_all, lam), to_blocks(l_dq)))
        o_mla = from_blocks(lax.map(lambda qs: mla_attend(qs[0], qs[1], kn_all, kr_all, v_all),
                                    (to_blocks(l_qn), to_blocks(l_qr))))
        x_new = x + ga * merge_heads(o_diff, o_mla, subln_w[l], lam_init, w_out[l])
        if not last:
            oc_diff = diff_attend(c_dq, c_dk, c_dv, lam)
            oc_mla = mla_attend(c_qn, c_qr, c_kn, c_kr, c_v)
            ctx = ctx + cga * merge_heads(oc_diff, oc_mla, subln_w[l], lam_init, w_out[l])
        x = x_new

        hf = modulate(rmsnorm(x, ffn_norm_w[l]), sf, scf)
        x = x + gf * moe(hf.reshape(B * n, D), router_w[l], router_b[l], w1[l], b1[l], w2[l], b2[l]).reshape(B, n, D)
        if not last:
            hc = modulate(rmsnorm(ctx, ffn_norm_w[l]), csf, cscf)
            ctx = ctx + cgf * moe(hc.reshape(-1, D), router_w[l], router_b[l], w1[l], b1[l], w2[l], b2[l]).reshape(ctx.shape)
    return rmsnorm(x, final_norm_w)
```

```python
import functools
import math

import jax
import jax.numpy as jnp
from jax import lax
from jax.experimental import pallas as pl
from jax.experimental.pallas import tpu as pltpu

F32 = jnp.float32
BF16 = jnp.bfloat16
I32 = jnp.int32

D_MODEL = 1024
GRID_W = 64
EPS = 1e-6
ROPE_THETA = 10000.0

HEADS = 4
DIFF_HEAD_DIM = 64
HEAD_V = 128
DIFF_SCALE = DIFF_HEAD_DIM ** -0.5
MLA_NOPE = 128
MLA_ROPE = 64
MLA_QK = 256
MLA_Q_RANK = 256
MLA_KV_RANK = 128
MLA_SCALE = (MLA_NOPE + MLA_ROPE) ** -0.5
ROPE_PAIRS = 16
GROUP_WIDTH = HEADS * HEAD_V

N_EXPERTS = 32
TOP_K = 4
D_FF = 1024
SWIGLU_LIMIT = 7.0
SWIGLU_ALPHA = 1.702

LANES = 128
VMEM_LIMIT = 56 * 1024 * 1024

PRE_TM = 512
ATT_TQ_DIFF = 256
ATT_TQ_MLA = 512
POST_TM = 512
DISP_TM = 512
EXP_TM = 512
COMB_TM = 256


def _cparams(sem, **kw):
    return pltpu.CompilerParams(dimension_semantics=sem, vmem_limit_bytes=VMEM_LIMIT, **kw)


def _rms(x, w):
    return x * lax.rsqrt(jnp.mean(x * x, axis=-1, keepdims=True) + EPS) * w


def _dot(a, b):
    return jnp.dot(a, b, preferred_element_type=F32)


def _dot_nt(a, b):
    return lax.dot_general(a, b, (((1,), (1,)), ((), ())), preferred_element_type=F32)


def _split_bf16(x):
    hi = x.astype(BF16)
    lo = (x - hi.astype(F32)).astype(BF16)
    return hi, lo


def _ada_kernel(c_ref, w_ref, b_ref, o_ref):
    c = c_ref[...]
    s = c / (1.0 + jnp.exp(-c))
    s_hi, s_lo = _split_bf16(s)
    w_hi, w_lo = _split_bf16(w_ref[...])
    o_ref[...] = _dot(s_hi, w_hi) + _dot(s_lo, w_hi) + _dot(s_hi, w_lo) + b_ref[...]


def _ada(cc, w, b):
    rows, d = cc.shape
    n_out = w.shape[1]
    tn = 1536
    return pl.pallas_call(
        _ada_kernel,
        out_shape=jax.ShapeDtypeStruct((rows, n_out), F32),
        grid=(n_out // tn,),
        in_specs=[pl.BlockSpec((rows, d), lambda j: (0, 0)),
                  pl.BlockSpec((d, tn), lambda j: (0, j)),
                  pl.BlockSpec((1, tn), lambda j: (0, j))],
        out_specs=pl.BlockSpec((rows, tn), lambda j: (0, j)),
        compiler_params=_cparams(("arbitrary",)),
        name="adaln",
    )(cc, w, b)


C_DK, C_DV, C_CKV, C_KR = 0, 512, 1024, 1152
C_DQ, C_CQ, C_DKS, C_KRS, C_DQS = 1280, 1792, 2048, 2560, 2688
N_COLS_CTX = 1280
N_COLS = 3200


def _pre_kernel(with_q, *refs):
    if with_q:
        (x_ref, sh_ref, sc_ref, nw_ref, w_ref, qnw_ref, kvnw_ref, wuq_ref, wuqs_ref, wukv_ref,
         cd_ref, sd_ref, ck_ref, sk_ref,
         dkT_ref, dv_ref, mkT_ref, mv_ref, dq_ref, mq_ref) = refs
    else:
        (x_ref, sh_ref, sc_ref, nw_ref, w_ref, kvnw_ref, wukv_ref,
         dkT_ref, dv_ref, mkT_ref, mv_ref) = refs

    x = x_ref[0]
    h = _rms(x, nw_ref[...]) * (1.0 + sc_ref[0]) + sh_ref[0]
    hb = h.astype(BF16)

    def proj(c0, width):
        return _dot(hb, w_ref[:, c0:c0 + width])

    def rope(v, vs, c, s):
        return v * c + vs * s if with_q else v

    if with_q:
        cd, sd, ck, sk = cd_ref[...], sd_ref[...], ck_ref[...], sk_ref[...]
    else:
        cd = sd = ck = sk = None

    dk = proj(C_DK, GROUP_WIDTH)
    dks = proj(C_DKS, GROUP_WIDTH) if with_q else None
    dv = proj(C_DV, GROUP_WIDTH)
    for hd in range(HEADS):
        sl = slice(hd * HEAD_V, (hd + 1) * HEAD_V)
        kh = rope(dk[:, sl], dks[:, sl] if with_q else None, cd, sd)
        dkT_ref[0, hd, 0] = kh.T.astype(BF16)
        dv_ref[0, hd, 0] = dv[:, sl].astype(BF16)

    ckv = _rms(proj(C_CKV, MLA_KV_RANK), kvnw_ref[...]).astype(BF16)
    kv = _dot(ckv, wukv_ref[...])
    kr = rope(proj(C_KR, LANES), proj(C_KRS, LANES) if with_q else None, ck, sk)
    krT = kr.T.astype(BF16)
    for hd in range(HEADS):
        c0 = hd * 2 * HEAD_V
        mkT_ref[0, hd, 0, 0:MLA_NOPE, :] = kv[:, c0:c0 + MLA_NOPE].T.astype(BF16)
        mkT_ref[0, hd, 0, MLA_NOPE:MLA_QK, :] = krT
        mv_ref[0, hd, 0] = kv[:, c0 + MLA_NOPE:c0 + 2 * HEAD_V].astype(BF16)

    if with_q:
        dq = proj(C_DQ, GROUP_WIDTH)
        dqs = proj(C_DQS, GROUP_WIDTH)
        for hd in range(HEADS):
            sl = slice(hd * HEAD_V, (hd + 1) * HEAD_V)
            dq_ref[0, hd] = ((dq[:, sl] * cd + dqs[:, sl] * sd) * DIFF_SCALE).astype(BF16)
        cq = _rms(proj(C_CQ, MLA_Q_RANK), qnw_ref[...]).astype(BF16)
        q = _dot(cq, wuq_ref[...])
        qs = _dot(cq, wuqs_ref[...])
        for hd in range(HEADS):
            c0 = hd * MLA_QK
            lo = q[:, c0:c0 + MLA_NOPE]
            hi = q[:, c0 + MLA_NOPE:c0 + MLA_QK] * ck + qs[:, hd * LANES:(hd + 1) * LANES] * sk
            mq_ref[0, hd, :, 0:MLA_NOPE] = (lo * MLA_SCALE).astype(BF16)
            mq_ref[0, hd, :, MLA_NOPE:MLA_QK] = (hi * MLA_SCALE).astype(BF16)


def _pre(x, shift, scale, nw, wts, tables, tm):
    B, n, d = x.shape
    with_q = tables is not None
    nt = n // tm
    per_b = shift.shape[0] > 1
    mod_map = (lambda b, j: (b, 0, 0)) if per_b else (lambda b, j: (0, 0, 0))
    full = lambda shape: pl.BlockSpec(shape, lambda b, j: (0,) * len(shape))
    ncols = N_COLS if with_q else N_COLS_CTX
    w_all = wts["w_all"][:, :ncols] if not with_q else wts["w_all"]

    in_specs = [pl.BlockSpec((1, tm, d), lambda b, j: (b, j, 0)),
                pl.BlockSpec((1, 1, d), mod_map), pl.BlockSpec((1, 1, d), mod_map),
                full((1, d)), full((d, ncols))]
    args = [x, shift, scale, nw, w_all]
    if with_q:
        in_specs += [full((1, MLA_Q_RANK)), full((1, MLA_KV_RANK)),
                     full((MLA_Q_RANK, HEADS * MLA_QK)), full((MLA_Q_RANK, HEADS * LANES)),
                     full((MLA_KV_RANK, HEADS * 2 * HEAD_V))]
        args += [wts["qnw"], wts["kvnw"], wts["wuq"], wts["wuqs"], wts["wukv"]]
        in_specs += [pl.BlockSpec((tm, LANES), lambda b, j: (j, 0))] * 4
        args += list(tables)
    else:
        in_specs += [full((1, MLA_KV_RANK)), full((MLA_KV_RANK, HEADS * 2 * HEAD_V))]
        args += [wts["kvnw"], wts["wukv"]]

    kv_map = lambda b, j: (b, 0, j, 0, 0)
    out_shape = [jax.ShapeDtypeStruct((B, HEADS, nt, HEAD_V, tm), BF16),
                 jax.ShapeDtypeStruct((B, HEADS, nt, tm, HEAD_V), BF16),
                 jax.ShapeDtypeStruct((B, HEADS, nt, MLA_QK, tm), BF16),
                 jax.ShapeDtypeStruct((B, HEADS, nt, tm, HEAD_V), BF16)]
    out_specs = [pl.BlockSpec((1, HEADS, 1, HEAD_V, tm), kv_map),
                 pl.BlockSpec((1, HEADS, 1, tm, HEAD_V), kv_map),
                 pl.BlockSpec((1, HEADS, 1, MLA_QK, tm), kv_map),
                 pl.BlockSpec((1, HEADS, 1, tm, HEAD_V), kv_map)]
    if with_q:
        q_map = lambda b, j: (b, 0, j, 0)
        out_shape += [jax.ShapeDtypeStruct((B, HEADS, n, HEAD_V), BF16),
                      jax.ShapeDtypeStruct((B, HEADS, n, MLA_QK), BF16)]
        out_specs += [pl.BlockSpec((1, HEADS, tm, HEAD_V), q_map),
                      pl.BlockSpec((1, HEADS, tm, MLA_QK), q_map)]
    return pl.pallas_call(
        functools.partial(_pre_kernel, with_q),
        out_shape=out_shape,
        grid=(B, nt),
        in_specs=in_specs,
        out_specs=out_specs,
        compiler_params=_cparams(("arbitrary", "arbitrary")),
        name="pre_attn_latent" if with_q else "pre_attn_ctx",
    )(*args)


def _attn_kernel(diff, lam_init, n_chunks, *refs):
    if diff:
        (lam_ref, q_ref, kT_ref, v_ref, kTc_ref, vc_ref, subw_ref, o_ref, m_ref, l_ref, acc_ref) = refs
    else:
        (q_ref, kT_ref, v_ref, kTc_ref, vc_ref, o_ref, m_ref, l_ref, acc_ref) = refs
    q = q_ref[0, 0]
    tq = q.shape[0]
    if diff:
        lane = lax.broadcasted_iota(I32, q.shape, 1)
        zero = jnp.zeros_like(q)
        q = jnp.concatenate([jnp.where(lane < DIFF_HEAD_DIM, q, zero),
                             jnp.where(lane >= DIFF_HEAD_DIM, q, zero)], axis=0)

    m_ref[...] = jnp.full(m_ref.shape, -jnp.inf, F32)
    l_ref[...] = jnp.zeros(l_ref.shape, F32)
    acc_ref[...] = jnp.zeros(acc_ref.shape, F32)

    def update(kT, v):
        s = _dot(q, kT)
        m_prev = m_ref[...]
        m_new = jnp.maximum(m_prev, jnp.max(s, axis=1, keepdims=True))
        alpha = jnp.exp(m_prev - m_new)
        p = jnp.exp(s - m_new)
        l_ref[...] = alpha * l_ref[...] + jnp.sum(p, axis=1, keepdims=True)
        acc_ref[...] = alpha * acc_ref[...] + _dot(p.astype(BF16), v)
        m_ref[...] = m_new

    update(kTc_ref[0, 0, 0], vc_ref[0, 0, 0])

    def body(j, carry):
        update(kT_ref[0, 0, j], v_ref[0, 0, j])
        return carry

    lax.fori_loop(0, n_chunks, body, 0)

    o = acc_ref[...] / l_ref[...]
    if diff:
        o = o[:tq] - lam_ref[0] * o[tq:]
        o = _rms(o, subw_ref[...]) * (1.0 - lam_init)
    o_ref[0] = o.astype(BF16)


def _attention(diff, lam, lam_init, q, kT, v, kTc, vc, subw, tq):
    B, H, n, dk = q.shape
    n_chunks, tk = kT.shape[2], kT.shape[4]
    tc = kTc.shape[4]
    rows = 2 * tq if diff else tq
    bh = lambda b, h, i: (b, h, 0, 0, 0)
    in_specs = [pl.BlockSpec((1, 1, tq, dk), lambda b, h, i: (b, h, i, 0)),
                pl.BlockSpec((1, 1, n_chunks, dk, tk), bh),
                pl.BlockSpec((1, 1, n_chunks, tk, HEAD_V), bh),
                pl.BlockSpec((1, 1, 1, dk, tc), bh),
                pl.BlockSpec((1, 1, 1, tc, HEAD_V), bh)]
    args = [q, kT, v, kTc, vc]
    if diff:
        in_specs = [pl.BlockSpec(memory_space=pltpu.SMEM)] + in_specs + [
            pl.BlockSpec((1, HEAD_V), lambda b, h, i: (0, 0))]
        args = [lam] + args + [subw]
    return pl.pallas_call(
        functools.partial(_attn_kernel, diff, lam_init, n_chunks),
        out_shape=jax.ShapeDtypeStruct((B, n, H * HEAD_V), BF16),
        grid=(B, H, n // tq),
        in_specs=in_specs,
        out_specs=pl.BlockSpec((1, tq, HEAD_V), lambda b, h, i: (b, i, h)),
        scratch_shapes=[pltpu.VMEM((rows, 1), F32), pltpu.VMEM((rows, 1), F32),
                        pltpu.VMEM((rows, HEAD_V), F32)],
        compiler_params=_cparams(("arbitrary", "arbitrary", "arbitrary")),
        name="attn_diff" if diff else "attn_mla",
    )(*args)


def _post_kernel(od_ref, om_ref, x_ref, ga_ref, wout_ref, fw_ref, sf_ref, scf_ref, rwT_ref, rb_ref,
                 x1_ref, hf_ref, topi_ref, rank_ref, gate_ref, cnt_ref, carry_ref):
    i = pl.program_id(0)

    @pl.when(i == 0)
    def _():
        carry_ref[...] = jnp.zeros(carry_ref.shape, F32)

    mix = jnp.concatenate([od_ref[...], om_ref[...]], axis=1)
    x1 = x_ref[...] + ga_ref[0] * _dot(mix, wout_ref[...])
    x1_ref[...] = x1
    hf = _rms(x1, fw_ref[...]) * (1.0 + scf_ref[0]) + sf_ref[0]
    hf_ref[...] = hf

    h_hi, h_lo = _split_bf16(hf)
    w_hi, w_lo = _split_bf16(rwT_ref[...])
    logits = _dot_nt(w_hi, h_hi) + _dot_nt(w_hi, h_lo) + _dot_nt(w_lo, h_hi) + rb_ref[...]
    n_e, tm = logits.shape
    eidx = lax.broadcasted_iota(I32, (n_e, tm), 0)

    vals, idxs = [], []
    cur = logits
    for _ in range(TOP_K):
        mx = jnp.max(cur, axis=0, keepdims=True)
        idx = jnp.min(jnp.where(cur == mx, eidx, n_e), axis=0, keepdims=True)
        vals.append(mx)
        idxs.append(idx)
        cur = jnp.where(eidx == idx, -jnp.inf, cur)
    exps = [jnp.exp(v - vals[0]) for v in vals]
    denom = exps[0] + exps[1] + exps[2] + exps[3]
    gates = jnp.concatenate([e / denom for e in exps], axis=0)

    onehot = jnp.zeros((n_e, tm), F32)
    for idx in idxs:
        onehot = onehot + jnp.where(eidx == idx, 1.0, 0.0)
    r = lax.broadcasted_iota(I32, (tm, tm), 0)
    c = lax.broadcasted_iota(I32, (tm, tm), 1)
    before = jnp.where(r < c, 1.0, 0.0).astype(BF16)
    prefix = _dot(onehot.astype(BF16), before) + carry_ref[...]
    ranks = [jnp.sum(jnp.where(eidx == idx, prefix, 0.0), axis=0, keepdims=True) for idx in idxs]
    carry = carry_ref[...] + jnp.sum(onehot, axis=1, keepdims=True)
    carry_ref[...] = carry

    topi_ref[...] = jnp.concatenate(idxs, axis=0)
    rank_ref[...] = jnp.concatenate(ranks, axis=0).astype(I32)
    gpad = jnp.concatenate([gates, jnp.zeros((LANES - TOP_K, tm), F32)], axis=0)
    gate_ref[...] = gpad.T
    cnt_ref[...] = jnp.broadcast_to(carry, cnt_ref.shape)


def _post(od, om, x, ga, wout, fw, sf, scf, rwT, rb, n_per_batch, tm):
    T, d = x.shape
    nt = T // tm
    per_b = n_per_batch // tm
    tok = lambda w: pl.BlockSpec((tm, w), lambda i: (i, 0))
    mod = pl.BlockSpec((1, 1, d), lambda i: (i // per_b, 0, 0))
    full = lambda shape: pl.BlockSpec(shape, lambda i: (0,) * len(shape))
    kt = lambda: pl.BlockSpec((TOP_K, tm), lambda i: (0, i))
    return pl.pallas_call(
        _post_kernel,
        out_shape=[jax.ShapeDtypeStruct((T, d), F32), jax.ShapeDtypeStruct((T, d), F32),
                   jax.ShapeDtypeStruct((TOP_K, T), I32), jax.ShapeDtypeStruct((TOP_K, T), I32),
                   jax.ShapeDtypeStruct((T, LANES), F32),
                   jax.ShapeDtypeStruct((N_EXPERTS, LANES), F32)],
        grid=(nt,),
        in_specs=[tok(GROUP_WIDTH), tok(GROUP_WIDTH), tok(d), mod, full((d, d)), full((1, d)), mod, mod,
                  full((N_EXPERTS, d)), full((N_EXPERTS, 1))],
        out_specs=[tok(d), tok(d), kt(), kt(), tok(LANES), full((N_EXPERTS, LANES))],
        scratch_shapes=[pltpu.VMEM((N_EXPERTS, 1), F32)],
        compiler_params=_cparams(("arbitrary",)),
        name="post_attn_router",
    )(od, om, x, ga, wout, fw, sf, scf, rwT, rb)


def _row_copy(src_ref, src_row, dst_ref, dst_row, sem):
    return pltpu.make_async_copy(src_ref.at[pl.ds(src_row, 1), :], dst_ref.at[pl.ds(dst_row, 1), :], sem)


def _dispatch_kernel(tm, dest_ref, hf_ref, xs_in_ref, xs_ref, sem):
    del xs_in_ref

    def issue(t, carry):
        for k in range(TOP_K):
            _row_copy(hf_ref, t, xs_ref, dest_ref[0, 0, k * tm + t], sem).start()
        return carry

    lax.fori_loop(0, tm, issue, 0)

    def drain(t, carry):
        _row_copy(hf_ref, 0, xs_ref, 0, sem).wait()
        return carry

    lax.fori_loop(0, TOP_K * tm, drain, 0)


def _dispatch(dest_tiles, hf, n_rows, tm):
    T, d = hf.shape
    xs0 = jnp.zeros((n_rows, d), F32)
    return pl.pallas_call(
        functools.partial(_dispatch_kernel, tm),
        out_shape=jax.ShapeDtypeStruct((n_rows, d), F32),
        grid=(T // tm,),
        in_specs=[pl.BlockSpec((1, 1, TOP_K * tm), lambda i: (i, 0, 0), memory_space=pltpu.SMEM),
                  pl.BlockSpec((tm, d), lambda i: (i, 0)),
                  pl.BlockSpec(memory_space=pl.ANY)],
        out_specs=pl.BlockSpec(memory_space=pl.ANY),
        scratch_shapes=[pltpu.SemaphoreType.DMA],
        input_output_aliases={2: 0},
        compiler_params=_cparams(("arbitrary",)),
        name="moe_dispatch",
    )(dest_tiles, hf, xs0)


def _expert_kernel(te_ref, nu_ref, xs_ref, w1_ref, b1_ref, w2_ref, b2_ref, ys_ref):
    @pl.when(pl.program_id(0) < nu_ref[0])
    def _():
        x = xs_ref[...].astype(BF16)
        hid = _dot(x, w1_ref[0]) + b1_ref[0]
        g = jnp.minimum(hid[:, :D_FF], SWIGLU_LIMIT)
        u = jnp.clip(hid[:, D_FF:], -SWIGLU_LIMIT, SWIGLU_LIMIT)
        act = g / (1.0 + jnp.exp(-SWIGLU_ALPHA * g)) * (u + 1.0)
        ys_ref[...] = _dot(act.astype(BF16), w2_ref[0]) + b2_ref[0]


def _experts(tile_e, n_used, xs, w1, b1, w2, b2, tm):
    n_rows, d = xs.shape
    nt = n_rows // tm
    row = lambda i, te, nu: (jnp.minimum(i, nu[0] - 1), 0)
    ex = lambda i, te, nu: (te[jnp.minimum(i, nu[0] - 1)], 0, 0)
    return pl.pallas_call(
        _expert_kernel,
        out_shape=jax.ShapeDtypeStruct((n_rows, d), F32),
        grid_spec=pltpu.PrefetchScalarGridSpec(
            num_scalar_prefetch=2,
            grid=(nt,),
            in_specs=[pl.BlockSpec((tm, d), row),
                      pl.BlockSpec((1, d, 2 * D_FF), ex), pl.BlockSpec((1, 1, 2 * D_FF), ex),
                      pl.BlockSpec((1, D_FF, d), ex), pl.BlockSpec((1, 1, d), ex)],
            out_specs=pl.BlockSpec((tm, d), row)),
        compiler_params=_cparams(("arbitrary",)),
        name="moe_experts",
    )(tile_e, n_used, xs, w1, b1, w2, b2)


def _combine_kernel(tm, dest_ref, x1_ref, gate_ref, gf_ref, fnw_ref, ys_ref, o_ref, buf_ref, sem):
    def issue(t, carry):
        for k in range(TOP_K):
            _row_copy(ys_ref, dest_ref[0, 0, k * tm + t], buf_ref.at[k], t, sem).start()
        return carry

    lax.fori_loop(0, tm, issue, 0)

    def drain(t, carry):
        _row_copy(ys_ref, 0, buf_ref.at[0], 0, sem).wait()
        return carry

    lax.fori_loop(0, TOP_K * tm, drain, 0)

    g = gate_ref[...]
    moe = g[:, 0:1] * buf_ref[0]
    for k in range(1, TOP_K):
        moe = moe + g[:, k:k + 1] * buf_ref[k]
    x2 = x1_ref[...] + gf_ref[0] * moe
    o_ref[...] = _rms(x2, fnw_ref[...])


def _combine(dest_tiles, x1, gates, gf, fnw, ys, n_per_batch, tm):
    T, d = x1.shape
    per_b = n_per_batch // tm
    return pl.pallas_call(
        functools.partial(_combine_kernel, tm),
        out_shape=jax.ShapeDtypeStruct((T, d), F32),
        grid=(T // tm,),
        in_specs=[pl.BlockSpec((1, 1, TOP_K * tm), lambda i: (i, 0, 0), memory_space=pltpu.SMEM),
                  pl.BlockSpec((tm, d), lambda i: (i, 0)),
                  pl.BlockSpec((tm, LANES), lambda i: (i, 0)),
                  pl.BlockSpec((1, 1, d), lambda i: (i // per_b, 0, 0)),
                  pl.BlockSpec((1, d), lambda i: (0, 0)),
                  pl.BlockSpec(memory_space=pl.ANY)],
        out_specs=pl.BlockSpec((tm, d), lambda i: (i, 0)),
        scratch_shapes=[pltpu.VMEM((TOP_K, tm, d), F32), pltpu.SemaphoreType.DMA],
        compiler_params=_cparams(("arbitrary",)),
        name="moe_combine",
    )(dest_tiles, x1, gates, gf, fnw, ys)


def _rope_partner_perm(width):
    i = jnp.arange(width)
    blk = i // 64
    r = i % 64
    a, j, p = r // 32, (r // 16) % 2, r % 16
    return blk * 64 + a * 32 + (1 - j) * 16 + p


def _prep_weights(w_in, q_norm_w, kv_norm_w, w_uq, w_ukv):
    dq, dk, dv, cq, ckv, kr = jnp.split(w_in, [512, 1024, 1536, 1792, 1920], axis=1)
    d = w_in.shape[0]
    zeros64 = jnp.zeros((d, 64), F32)
    krp = jnp.concatenate([kr, zeros64], axis=1)
    perm512 = _rope_partner_perm(512)
    krs = jnp.concatenate([kr[:, _rope_partner_perm(64)], zeros64], axis=1)
    w_all = jnp.concatenate([dk, dv, ckv, krp, dq, cq, dk[:, perm512], krs, dq[:, perm512]], axis=1)
    wq = w_uq.reshape(MLA_Q_RANK, HEADS, MLA_NOPE + MLA_ROPE)
    z = jnp.zeros((MLA_Q_RANK, HEADS, 64), F32)
    wuq = jnp.concatenate([wq, z], axis=2).reshape(MLA_Q_RANK, HEADS * MLA_QK)
    wq_rope_sw = wq[:, :, MLA_NOPE:][:, :, _rope_partner_perm(64)]
    wuqs = jnp.concatenate([wq_rope_sw, z], axis=2).reshape(MLA_Q_RANK, HEADS * LANES)
    return {"w_all": w_all.astype(BF16), "qnw": q_norm_w[None, :], "kvnw": kv_norm_w[None, :],
            "wuq": wuq.astype(BF16), "wuqs": wuqs.astype(BF16), "wukv": w_ukv.astype(BF16)}


def _rope_tables(n):
    rows = n // GRID_W
    row = jnp.repeat(jnp.arange(rows, dtype=F32), GRID_W)
    col = jnp.tile(jnp.arange(GRID_W, dtype=F32), rows)
    inv = ROPE_THETA ** (-jnp.arange(ROPE_PAIRS, dtype=F32) / ROPE_PAIRS)
    ang = jnp.stack([row[:, None] * inv, col[:, None] * inv], axis=1)
    cos, sin = jnp.cos(ang), jnp.sin(ang)
    c64 = jnp.stack([cos, cos], axis=2).reshape(n, 64)
    s64 = jnp.stack([-sin, sin], axis=2).reshape(n, 64)
    one, zero = jnp.ones((n, 64), F32), jnp.zeros((n, 64), F32)
    return (jnp.concatenate([c64, c64], axis=1), jnp.concatenate([s64, s64], axis=1),
            jnp.concatenate([c64, one], axis=1), jnp.concatenate([s64, zero], axis=1))


def kernel(x, c, ctx, c_ctx, w_ada, b_ada, attn_norm_w, w_in, q_norm_w, kv_norm_w, w_uq, w_ukv,
           lambda_q1, lambda_k1, lambda_q2, lambda_k2, subln_w, w_out, ffn_norm_w,
           router_w, router_b, w1, b1, w2, b2, final_norm_w):
    B, n, d = x.shape
    T = B * n
    depth = w_ada.shape[0]
    assert depth == 1 and d == D_MODEL and n % PRE_TM == 0 and B + 1 <= 8
    l = 0
    lam_init = 0.8 - 0.6 * math.exp(-0.3 * l)
    lam = (jnp.exp(jnp.sum(lambda_q1[l] * lambda_k1[l])) - jnp.exp(jnp.sum(lambda_q2[l] * lambda_k2[l]))
           + lam_init).reshape(1).astype(F32)

    cc = jnp.concatenate([c, c_ctx[None, :], jnp.zeros((8 - B - 1, d), F32)], axis=0)
    mod = _ada(cc, w_ada[l], b_ada[l][None, :])
    sa, sca, ga, sf, scf, gf = [mod[:B, i * d:(i + 1) * d][:, None, :] for i in range(6)]
    csa, csca = [mod[B:B + 1, i * d:(i + 1) * d][:, None, :] for i in range(2)]

    wts = _prep_weights(w_in[l], q_norm_w[l], kv_norm_w[l], w_uq[l], w_ukv[l])
    anw = attn_norm_w[l][None, :]
    dkT, dv, mkT, mv, dq, mq = _pre(x, sa, sca, anw, wts, _rope_tables(n), PRE_TM)
    dkTc, dvc, mkTc, mvc = _pre(ctx, csa, csca, anw, wts, None, ctx.shape[1])

    o_diff = _attention(True, lam, lam_init, dq, dkT, dv, dkTc, dvc, subln_w[l][None, :], ATT_TQ_DIFF)
    o_mla = _attention(False, None, lam_init, mq, mkT, mv, mkTc, mvc, None, ATT_TQ_MLA)

    x1, hf, topi, rank, gates, cnt = _post(
        o_diff.reshape(T, GROUP_WIDTH), o_mla.reshape(T, GROUP_WIDTH), x.reshape(T, d), ga,
        w_out[l].astype(BF16), ffn_norm_w[l][None, :], sf, scf,
        router_w[l].T, router_b[l][:, None], n, POST_TM)

    counts = cnt[:, 0].astype(I32)
    padded = ((counts + EXP_TM - 1) // EXP_TM) * EXP_TM
    ends = jnp.cumsum(padded)
    pad_start = ends - padded
    n_tiles = (T * TOP_K) // EXP_TM + N_EXPERTS
    e_ids = jnp.arange(N_EXPERTS, dtype=I32)
    dest = jnp.sum(jnp.where(topi[:, :, None] == e_ids, pad_start, 0), axis=-1).astype(I32) + rank
    tile_e = jnp.minimum(jnp.searchsorted(ends, jnp.arange(n_tiles, dtype=I32) * EXP_TM, side="right"),
                         N_EXPERTS - 1).astype(I32)
    n_used = (ends[-1:] // EXP_TM).astype(I32)

    def tiles(tm):
        return dest.reshape(TOP_K, T // tm, tm).transpose(1, 0, 2).reshape(T // tm, 1, TOP_K * tm)

    xs = _dispatch(tiles(DISP_TM), hf, n_tiles * EXP_TM, DISP_TM)
    ys = _experts(tile_e, n_used, xs, w1[l].astype(BF16), b1[l][:, None, :],
                  w2[l].astype(BF16), b2[l][:, None, :], EXP_TM)
    out = _combine(tiles(COMB_TM), x1, gates, gf, final_norm_w[None, :], ys, n, COMB_TM)
    return out.reshape(B, n, d)
```

```python
import functools
import math

import jax
import jax.numpy as jnp
from jax import lax
from jax.experimental import pallas as pl
from jax.experimental.pallas import tpu as pltpu

F32 = jnp.float32
BF16 = jnp.bfloat16
I32 = jnp.int32

D_MODEL = 1024
GRID_W = 64
EPS = 1e-6
ROPE_THETA = 10000.0

HEADS = 4
DIFF_HEAD_DIM = 64
HEAD_V = 128
DIFF_SCALE = DIFF_HEAD_DIM ** -0.5
MLA_NOPE = 128
MLA_ROPE = 64
MLA_QK = 256
MLA_Q_RANK = 256
MLA_KV_RANK = 128
MLA_SCALE = (MLA_NOPE + MLA_ROPE) ** -0.5
LOG2E = math.log2(math.e)
ROPE_PAIRS = 16
GROUP_WIDTH = HEADS * HEAD_V

N_EXPERTS = 32
TOP_K = 4
D_FF = 1024
SWIGLU_LIMIT = 7.0
SWIGLU_ALPHA = 1.702

LANES = 128
VMEM_LIMIT = 56 * 1024 * 1024

PRE_TM = 512
ATT_TQ_DIFF = 256
ATT_TQ_MLA = 512
POST_TM = 512
DISP_TM = 512
EXP_TM = 512
COMB_TM = 256


def _cparams(sem, **kw):
    return pltpu.CompilerParams(dimension_semantics=sem, vmem_limit_bytes=VMEM_LIMIT, **kw)


def _rms(x, w):
    return x * lax.rsqrt(jnp.mean(x * x, axis=-1, keepdims=True) + EPS) * w


def _dot(a, b):
    return jnp.dot(a, b, preferred_element_type=F32)


def _dot_nt(a, b):
    return lax.dot_general(a, b, (((1,), (1,)), ((), ())), preferred_element_type=F32)


def _split_bf16(x):
    hi = x.astype(BF16)
    lo = (x - hi.astype(F32)).astype(BF16)
    return hi, lo


def _ada_kernel(c_ref, w_ref, b_ref, o_ref):
    c = c_ref[...]
    s = c / (1.0 + jnp.exp(-c))
    s_hi, s_lo = _split_bf16(s)
    w_hi, w_lo = _split_bf16(w_ref[...])
    o_ref[...] = _dot(s_hi, w_hi) + _dot(s_lo, w_hi) + _dot(s_hi, w_lo) + b_ref[...]


def _ada(cc, w, b):
    rows, d = cc.shape
    n_out = w.shape[1]
    tn = 1536
    return pl.pallas_call(
        _ada_kernel,
        out_shape=jax.ShapeDtypeStruct((rows, n_out), F32),
        grid=(n_out // tn,),
        in_specs=[pl.BlockSpec((rows, d), lambda j: (0, 0)),
                  pl.BlockSpec((d, tn), lambda j: (0, j)),
                  pl.BlockSpec((1, tn), lambda j: (0, j))],
        out_specs=pl.BlockSpec((rows, tn), lambda j: (0, j)),
        compiler_params=_cparams(("arbitrary",)),
        name="adaln",
    )(cc, w, b)


C_DK, C_DV, C_CKV, C_KR = 0, 512, 1024, 1152
C_DQ, C_CQ, C_DKS, C_KRS, C_DQS = 1280, 1792, 2048, 2560, 2688
N_COLS_CTX = 1280
N_COLS = 3200


def _pre_kernel(with_q, *refs):
    if with_q:
        (x_ref, sh_ref, sc_ref, nw_ref, w_ref, qnw_ref, kvnw_ref, wuq_ref, wuqs_ref, wukv_ref,
         cd_ref, sd_ref, ck_ref, sk_ref,
         dk_ref, dvT_ref, mk_ref, mvT_ref, dqT_ref, mqT_ref) = refs
    else:
        (x_ref, sh_ref, sc_ref, nw_ref, w_ref, kvnw_ref, wukv_ref,
         dk_ref, dvT_ref, mk_ref, mvT_ref) = refs

    x = x_ref[0]
    h = _rms(x, nw_ref[...]) * (1.0 + sc_ref[0]) + sh_ref[0]
    hb = h.astype(BF16)

    def proj(c0, width):
        return _dot(hb, w_ref[:, c0:c0 + width])

    def rope(v, vs, c, s):
        return v * c + vs * s if with_q else v

    if with_q:
        cd, sd, ck, sk = cd_ref[...], sd_ref[...], ck_ref[...], sk_ref[...]
    else:
        cd = sd = ck = sk = None

    dk = proj(C_DK, GROUP_WIDTH)
    dks = proj(C_DKS, GROUP_WIDTH) if with_q else None
    dv = proj(C_DV, GROUP_WIDTH)
    for hd in range(HEADS):
        sl = slice(hd * HEAD_V, (hd + 1) * HEAD_V)
        dk_ref[0, hd] = rope(dk[:, sl], dks[:, sl] if with_q else None, cd, sd).astype(BF16)
        dvT_ref[0, hd, 0] = dv[:, sl].T.astype(BF16)

    ckv = _rms(proj(C_CKV, MLA_KV_RANK), kvnw_ref[...]).astype(BF16)
    kv = _dot(ckv, wukv_ref[...])
    kr = rope(proj(C_KR, LANES), proj(C_KRS, LANES) if with_q else None, ck, sk).astype(BF16)
    for hd in range(HEADS):
        c0 = hd * 2 * HEAD_V
        mk_ref[0, hd, :, 0:MLA_NOPE] = kv[:, c0:c0 + MLA_NOPE].astype(BF16)
        mk_ref[0, hd, :, MLA_NOPE:MLA_QK] = kr
        mvT_ref[0, hd, 0] = kv[:, c0 + MLA_NOPE:c0 + 2 * HEAD_V].T.astype(BF16)

    if with_q:
        dq = proj(C_DQ, GROUP_WIDTH)
        dqs = proj(C_DQS, GROUP_WIDTH)
        for hd in range(HEADS):
            sl = slice(hd * HEAD_V, (hd + 1) * HEAD_V)
            dqT_ref[0, hd] = ((dq[:, sl] * cd + dqs[:, sl] * sd) * (DIFF_SCALE * LOG2E)).T.astype(BF16)
        cq = _rms(proj(C_CQ, MLA_Q_RANK), qnw_ref[...]).astype(BF16)
        q = _dot(cq, wuq_ref[...])
        qs = _dot(cq, wuqs_ref[...])
        for hd in range(HEADS):
            c0 = hd * MLA_QK
            lo = q[:, c0:c0 + MLA_NOPE]
            hi = q[:, c0 + MLA_NOPE:c0 + MLA_QK] * ck + qs[:, hd * LANES:(hd + 1) * LANES] * sk
            mqT_ref[0, hd, 0:MLA_NOPE, :] = (lo * (MLA_SCALE * LOG2E)).T.astype(BF16)
            mqT_ref[0, hd, MLA_NOPE:MLA_QK, :] = (hi * (MLA_SCALE * LOG2E)).T.astype(BF16)


def _pre(x, shift, scale, nw, wts, tables, tm):
    B, n, d = x.shape
    with_q = tables is not None
    nt = n // tm
    per_b = shift.shape[0] > 1
    mod_map = (lambda b, j: (b, 0, 0)) if per_b else (lambda b, j: (0, 0, 0))
    full = lambda shape: pl.BlockSpec(shape, lambda b, j: (0,) * len(shape))
    ncols = N_COLS if with_q else N_COLS_CTX
    w_all = wts["w_all"] if with_q else wts["w_all"][:, :ncols]

    in_specs = [pl.BlockSpec((1, tm, d), lambda b, j: (b, j, 0)),
                pl.BlockSpec((1, 1, d), mod_map), pl.BlockSpec((1, 1, d), mod_map),
                full((1, d)), full((d, ncols))]
    args = [x, shift, scale, nw, w_all]
    if with_q:
        in_specs += [full((1, MLA_Q_RANK)), full((1, MLA_KV_RANK)),
                     full((MLA_Q_RANK, HEADS * MLA_QK)), full((MLA_Q_RANK, HEADS * LANES)),
                     full((MLA_KV_RANK, HEADS * 2 * HEAD_V))]
        args += [wts["qnw"], wts["kvnw"], wts["wuq"], wts["wuqs"], wts["wukv"]]
        in_specs += [pl.BlockSpec((tm, LANES), lambda b, j: (j, 0))] * 4
        args += list(tables)
    else:
        in_specs += [full((1, MLA_KV_RANK)), full((MLA_KV_RANK, HEADS * 2 * HEAD_V))]
        args += [wts["kvnw"], wts["wukv"]]

    row_map = lambda b, j: (b, 0, j, 0)
    chunk_map = lambda b, j: (b, 0, j, 0, 0)
    out_shape = [jax.ShapeDtypeStruct((B, HEADS, n, HEAD_V), BF16),
                 jax.ShapeDtypeStruct((B, HEADS, nt, HEAD_V, tm), BF16),
                 jax.ShapeDtypeStruct((B, HEADS, n, MLA_QK), BF16),
                 jax.ShapeDtypeStruct((B, HEADS, nt, HEAD_V, tm), BF16)]
    out_specs = [pl.BlockSpec((1, HEADS, tm, HEAD_V), row_map),
                 pl.BlockSpec((1, HEADS, 1, HEAD_V, tm), chunk_map),
                 pl.BlockSpec((1, HEADS, tm, MLA_QK), row_map),
                 pl.BlockSpec((1, HEADS, 1, HEAD_V, tm), chunk_map)]
    if with_q:
        col_map = lambda b, j: (b, 0, 0, j)
        out_shape += [jax.ShapeDtypeStruct((B, HEADS, HEAD_V, n), BF16),
                      jax.ShapeDtypeStruct((B, HEADS, MLA_QK, n), BF16)]
        out_specs += [pl.BlockSpec((1, HEADS, HEAD_V, tm), col_map),
                      pl.BlockSpec((1, HEADS, MLA_QK, tm), col_map)]
    return pl.pallas_call(
        functools.partial(_pre_kernel, with_q),
        out_shape=out_shape,
        grid=(B, nt),
        in_specs=in_specs,
        out_specs=out_specs,
        compiler_params=_cparams(("arbitrary", "arbitrary")),
        name="pre_attn_latent" if with_q else "pre_attn_ctx",
    )(*args)


def _attn_kernel(diff, lam_init, n_chunks, tk, *refs):
    if diff:
        (lam_ref, qT_ref, k_ref, vT_ref, kc_ref, vTc_ref, subw_ref, o_ref, s_ref, m_ref, l_ref, acc_ref) = refs
    else:
        (qT_ref, k_ref, vT_ref, kc_ref, vTc_ref, o_ref, s_ref, m_ref, l_ref, acc_ref) = refs
    qT = qT_ref[0, 0]
    tq = qT.shape[1]
    if diff:
        row = lax.broadcasted_iota(I32, qT.shape, 0)
        zero = jnp.zeros_like(qT)
        qT = jnp.concatenate([jnp.where(row < DIFF_HEAD_DIM, qT, zero),
                              jnp.where(row >= DIFF_HEAD_DIM, qT, zero)], axis=1)

    def softmax_pv(s, vT, first):
        m_prev = m_ref[...]
        m_new = jnp.max(s, axis=0, keepdims=True)
        if not first:
            m_new = jnp.maximum(m_prev, m_new)
        p = jnp.exp2(s - m_new)
        psum = jnp.sum(p, axis=0, keepdims=True)
        pv = _dot(vT, p.astype(BF16))
        if first:
            l_ref[...] = psum
            acc_ref[...] = pv
        else:
            alpha = jnp.exp2(m_prev - m_new)
            l_ref[...] = alpha * l_ref[...] + psum
            acc_ref[...] = alpha * acc_ref[...] + pv
        m_ref[...] = m_new

    def scores(j):
        return _dot(k_ref[0, 0, pl.ds(pl.multiple_of(j * tk, tk), tk), :], qT)

    s_ref[...] = scores(0)
    softmax_pv(_dot(kc_ref[0, 0], qT), vTc_ref[0, 0, 0], True)

    def body(j, carry):
        s_next = scores(j + 1)
        softmax_pv(s_ref[...], vT_ref[0, 0, j], False)
        s_ref[...] = s_next
        return carry

    lax.fori_loop(0, n_chunks - 1, body, 0)
    softmax_pv(s_ref[...], vT_ref[0, 0, n_chunks - 1], False)

    o = acc_ref[...] / l_ref[...]
    if diff:
        o = o[:, :tq] - lam_ref[0] * o[:, tq:]
        o = o * lax.rsqrt(jnp.mean(o * o, axis=0, keepdims=True) + EPS) * subw_ref[...] * (1.0 - lam_init)
    o_ref[0] = o.T.astype(BF16)


def _attention(diff, lam, lam_init, qT, k, vT, kc, vTc, subw, tq):
    B, H, dk, n = qT.shape
    n_chunks, tk = vT.shape[2], vT.shape[4]
    tc = vTc.shape[4]
    cols = 2 * tq if diff else tq
    bh4 = lambda b, h, i: (b, h, 0, 0)
    bh5 = lambda b, h, i: (b, h, 0, 0, 0)
    in_specs = [pl.BlockSpec((1, 1, dk, tq), lambda b, h, i: (b, h, 0, i)),
                pl.BlockSpec((1, 1, n, dk), bh4),
                pl.BlockSpec((1, 1, n_chunks, HEAD_V, tk), bh5),
                pl.BlockSpec((1, 1, tc, dk), bh4),
                pl.BlockSpec((1, 1, 1, HEAD_V, tc), bh5)]
    args = [qT, k, vT, kc, vTc]
    if diff:
        in_specs = [pl.BlockSpec(memory_space=pltpu.SMEM)] + in_specs + [
            pl.BlockSpec((HEAD_V, 1), lambda b, h, i: (0, 0))]
        args = [lam] + args + [subw]
    return pl.pallas_call(
        functools.partial(_attn_kernel, diff, lam_init, n_chunks, tk),
        out_shape=jax.ShapeDtypeStruct((B, n, H * HEAD_V), BF16),
        grid=(B, H, n // tq),
        in_specs=in_specs,
        out_specs=pl.BlockSpec((1, tq, HEAD_V), lambda b, h, i: (b, i, h)),
        scratch_shapes=[pltpu.VMEM((tk, cols), F32), pltpu.VMEM((1, cols), F32), pltpu.VMEM((1, cols), F32),
                        pltpu.VMEM((HEAD_V, cols), F32)],
        compiler_params=_cparams(("arbitrary", "arbitrary", "arbitrary")),
        name="attn_diff" if diff else "attn_mla",
    )(*args)


def _post_kernel(od_ref, om_ref, x_ref, ga_ref, wout_ref, fw_ref, sf_ref, scf_ref, rwT_ref, rb_ref,
                 x1_ref, hf_ref, topi_ref, rank_ref, gate_ref, cnt_ref, carry_ref):
    i = pl.program_id(0)

    @pl.when(i == 0)
    def _():
        carry_ref[...] = jnp.zeros(carry_ref.shape, F32)

    mix = jnp.concatenate([od_ref[...], om_ref[...]], axis=1)
    x1 = x_ref[...] + ga_ref[0] * _dot(mix, wout_ref[...])
    x1_ref[...] = x1
    hf = _rms(x1, fw_ref[...]) * (1.0 + scf_ref[0]) + sf_ref[0]
    hf_ref[...] = hf

    h_hi, h_lo = _split_bf16(hf)
    w_hi, w_lo = _split_bf16(rwT_ref[...])
    logits = _dot_nt(w_hi, h_hi) + _dot_nt(w_hi, h_lo) + _dot_nt(w_lo, h_hi) + rb_ref[...]
    n_e, tm = logits.shape
    eidx = lax.broadcasted_iota(I32, (n_e, tm), 0)

    vals, idxs = [], []
    cur = logits
    for _ in range(TOP_K):
        mx = jnp.max(cur, axis=0, keepdims=True)
        idx = jnp.min(jnp.where(cur == mx, eidx, n_e), axis=0, keepdims=True)
        vals.append(mx)
        idxs.append(idx)
        cur = jnp.where(eidx == idx, -jnp.inf, cur)
    exps = [jnp.exp(v - vals[0]) for v in vals]
    denom = exps[0] + exps[1] + exps[2] + exps[3]
    gates = jnp.concatenate([e / denom for e in exps], axis=0)

    onehot = jnp.zeros((n_e, tm), F32)
    for idx in idxs:
        onehot = onehot + jnp.where(eidx == idx, 1.0, 0.0)
    r = lax.broadcasted_iota(I32, (tm, tm), 0)
    c = lax.broadcasted_iota(I32, (tm, tm), 1)
    before = jnp.where(r < c, 1.0, 0.0).astype(BF16)
    prefix = _dot(onehot.astype(BF16), before) + carry_ref[...]
    ranks = [jnp.sum(jnp.where(eidx == idx, prefix, 0.0), axis=0, keepdims=True) for idx in idxs]
    carry = carry_ref[...] + jnp.sum(onehot, axis=1, keepdims=True)
    carry_ref[...] = carry

    topi_ref[...] = jnp.concatenate(idxs, axis=0)
    rank_ref[...] = jnp.concatenate(ranks, axis=0).astype(I32)
    gpad = jnp.concatenate([gates, jnp.zeros((LANES - TOP_K, tm), F32)], axis=0)
    gate_ref[...] = gpad.T
    cnt_ref[...] = jnp.broadcast_to(carry, cnt_ref.shape)


def _post(od, om, x, ga, wout, fw, sf, scf, rwT, rb, n_per_batch, tm):
    T, d = x.shape
    nt = T // tm
    per_b = n_per_batch // tm
    tok = lambda w: pl.BlockSpec((tm, w), lambda i: (i, 0))
    mod = pl.BlockSpec((1, 1, d), lambda i: (i // per_b, 0, 0))
    full = lambda shape: pl.BlockSpec(shape, lambda i: (0,) * len(shape))
    kt = lambda: pl.BlockSpec((TOP_K, tm), lambda i: (0, i))
    return pl.pallas_call(
        _post_kernel,
        out_shape=[jax.ShapeDtypeStruct((T, d), F32), jax.ShapeDtypeStruct((T, d), F32),
                   jax.ShapeDtypeStruct((TOP_K, T), I32), jax.ShapeDtypeStruct((TOP_K, T), I32),
                   jax.ShapeDtypeStruct((T, LANES), F32),
                   jax.ShapeDtypeStruct((N_EXPERTS, LANES), F32)],
        grid=(nt,),
        in_specs=[tok(GROUP_WIDTH), tok(GROUP_WIDTH), tok(d), mod, full((d, d)), full((1, d)), mod, mod,
                  full((N_EXPERTS, d)), full((N_EXPERTS, 1))],
        out_specs=[tok(d), tok(d), kt(), kt(), tok(LANES), full((N_EXPERTS, LANES))],
        scratch_shapes=[pltpu.VMEM((N_EXPERTS, 1), F32)],
        compiler_params=_cparams(("arbitrary",)),
        name="post_attn_router",
    )(od, om, x, ga, wout, fw, sf, scf, rwT, rb)


def _row_copy(src_ref, src_row, dst_ref, dst_row, sem):
    return pltpu.make_async_copy(src_ref.at[pl.ds(src_row, 1), :], dst_ref.at[pl.ds(dst_row, 1), :], sem)


def _dispatch_kernel(tm, dest_ref, hf_ref, xs_in_ref, xs_ref, sem):
    del xs_in_ref

    def issue(t, carry):
        for k in range(TOP_K):
            _row_copy(hf_ref, t, xs_ref, dest_ref[0, 0, k * tm + t], sem).start()
        return carry

    lax.fori_loop(0, tm, issue, 0)

    def drain(t, carry):
        _row_copy(hf_ref, 0, xs_ref, 0, sem).wait()
        return carry

    lax.fori_loop(0, TOP_K * tm, drain, 0)


def _dispatch(dest_tiles, hf, n_rows, tm):
    T, d = hf.shape
    xs0 = jnp.zeros((n_rows, d), F32)
    return pl.pallas_call(
        functools.partial(_dispatch_kernel, tm),
        out_shape=jax.ShapeDtypeStruct((n_rows, d), F32),
        grid=(T // tm,),
        in_specs=[pl.BlockSpec((1, 1, TOP_K * tm), lambda i: (i, 0, 0), memory_space=pltpu.SMEM),
                  pl.BlockSpec((tm, d), lambda i: (i, 0)),
                  pl.BlockSpec(memory_space=pl.ANY)],
        out_specs=pl.BlockSpec(memory_space=pl.ANY),
        scratch_shapes=[pltpu.SemaphoreType.DMA],
        input_output_aliases={2: 0},
        compiler_params=_cparams(("arbitrary",)),
        name="moe_dispatch",
    )(dest_tiles, hf, xs0)


def _expert_kernel(te_ref, nu_ref, xs_ref, w1_ref, b1_ref, w2_ref, b2_ref, ys_ref):
    @pl.when(pl.program_id(0) < nu_ref[0])
    def _():
        x = xs_ref[...].astype(BF16)
        hid = _dot(x, w1_ref[0]) + b1_ref[0]
        g = jnp.minimum(hid[:, :D_FF], SWIGLU_LIMIT)
        u = jnp.clip(hid[:, D_FF:], -SWIGLU_LIMIT, SWIGLU_LIMIT)
        act = g / (1.0 + jnp.exp(-SWIGLU_ALPHA * g)) * (u + 1.0)
        ys_ref[...] = _dot(act.astype(BF16), w2_ref[0]) + b2_ref[0]


def _experts(tile_e, n_used, xs, w1, b1, w2, b2, tm):
    n_rows, d = xs.shape
    nt = n_rows // tm
    row = lambda i, te, nu: (jnp.minimum(i, nu[0] - 1), 0)
    ex = lambda i, te, nu: (te[jnp.minimum(i, nu[0] - 1)], 0, 0)
    return pl.pallas_call(
        _expert_kernel,
        out_shape=jax.ShapeDtypeStruct((n_rows, d), F32),
        grid_spec=pltpu.PrefetchScalarGridSpec(
            num_scalar_prefetch=2,
            grid=(nt,),
            in_specs=[pl.BlockSpec((tm, d), row),
                      pl.BlockSpec((1, d, 2 * D_FF), ex), pl.BlockSpec((1, 1, 2 * D_FF), ex),
                      pl.BlockSpec((1, D_FF, d), ex), pl.BlockSpec((1, 1, d), ex)],
            out_specs=pl.BlockSpec((tm, d), row)),
        compiler_params=_cparams(("arbitrary",)),
        name="moe_experts",
    )(tile_e, n_used, xs, w1, b1, w2, b2)


def _combine_kernel(tm, dest_ref, x1_ref, gate_ref, gf_ref, fnw_ref, ys_ref, o_ref, buf_ref, sem):
    def issue(t, carry):
        for k in range(TOP_K):
            _row_copy(ys_ref, dest_ref[0, 0, k * tm + t], buf_ref.at[k], t, sem).start()
        return carry

    lax.fori_loop(0, tm, issue, 0)

    def drain(t, carry):
        _row_copy(ys_ref, 0, buf_ref.at[0], 0, sem).wait()
        return carry

    lax.fori_loop(0, TOP_K * tm, drain, 0)

    g = gate_ref[...]
    moe = g[:, 0:1] * buf_ref[0]
    for k in range(1, TOP_K):
        moe = moe + g[:, k:k + 1] * buf_ref[k]
    x2 = x1_ref[...] + gf_ref[0] * moe
    o_ref[...] = _rms(x2, fnw_ref[...])


def _combine(dest_tiles, x1, gates, gf, fnw, ys, n_per_batch, tm):
    T, d = x1.shape
    per_b = n_per_batch // tm
    return pl.pallas_call(
        functools.partial(_combine_kernel, tm),
        out_shape=jax.ShapeDtypeStruct((T, d), F32),
        grid=(T // tm,),
        in_specs=[pl.BlockSpec((1, 1, TOP_K * tm), lambda i: (i, 0, 0), memory_space=pltpu.SMEM),
                  pl.BlockSpec((tm, d), lambda i: (i, 0)),
                  pl.BlockSpec((tm, LANES), lambda i: (i, 0)),
                  pl.BlockSpec((1, 1, d), lambda i: (i // per_b, 0, 0)),
                  pl.BlockSpec((1, d), lambda i: (0, 0)),
                  pl.BlockSpec(memory_space=pl.ANY)],
        out_specs=pl.BlockSpec((tm, d), lambda i: (i, 0)),
        scratch_shapes=[pltpu.VMEM((TOP_K, tm, d), F32), pltpu.SemaphoreType.DMA],
        compiler_params=_cparams(("arbitrary",)),
        name="moe_combine",
    )(dest_tiles, x1, gates, gf, fnw, ys)


def _rope_partner_perm(width):
    i = jnp.arange(width)
    blk = i // 64
    r = i % 64
    a, j, p = r // 32, (r // 16) % 2, r % 16
    return blk * 64 + a * 32 + (1 - j) * 16 + p


def _prep_weights(w_in, q_norm_w, kv_norm_w, w_uq, w_ukv):
    dq, dk, dv, cq, ckv, kr = jnp.split(w_in, [512, 1024, 1536, 1792, 1920], axis=1)
    d = w_in.shape[0]
    zeros64 = jnp.zeros((d, 64), F32)
    krp = jnp.concatenate([kr, zeros64], axis=1)
    perm512 = _rope_partner_perm(512)
    krs = jnp.concatenate([kr[:, _rope_partner_perm(64)], zeros64], axis=1)
    w_all = jnp.concatenate([dk, dv, ckv, krp, dq, cq, dk[:, perm512], krs, dq[:, perm512]], axis=1)
    wq = w_uq.reshape(MLA_Q_RANK, HEADS, MLA_NOPE + MLA_ROPE)
    z = jnp.zeros((MLA_Q_RANK, HEADS, 64), F32)
    wuq = jnp.concatenate([wq, z], axis=2).reshape(MLA_Q_RANK, HEADS * MLA_QK)
    wq_rope_sw = wq[:, :, MLA_NOPE:][:, :, _rope_partner_perm(64)]
    wuqs = jnp.concatenate([wq_rope_sw, z], axis=2).reshape(MLA_Q_RANK, HEADS * LANES)
    return {"w_all": w_all.astype(BF16), "qnw": q_norm_w[None, :], "kvnw": kv_norm_w[None, :],
            "wuq": wuq.astype(BF16), "wuqs": wuqs.astype(BF16), "wukv": w_ukv.astype(BF16)}


def _rope_tables(n):
    rows = n // GRID_W
    row = jnp.repeat(jnp.arange(rows, dtype=F32), GRID_W)
    col = jnp.tile(jnp.arange(GRID_W, dtype=F32), rows)
    inv = ROPE_THETA ** (-jnp.arange(ROPE_PAIRS, dtype=F32) / ROPE_PAIRS)
    ang = jnp.stack([row[:, None] * inv, col[:, None] * inv], axis=1)
    cos, sin = jnp.cos(ang), jnp.sin(ang)
    c64 = jnp.stack([cos, cos], axis=2).reshape(n, 64)
    s64 = jnp.stack([-sin, sin], axis=2).reshape(n, 64)
    one, zero = jnp.ones((n, 64), F32), jnp.zeros((n, 64), F32)
    return (jnp.concatenate([c64, c64], axis=1), jnp.concatenate([s64, s64], axis=1),
            jnp.concatenate([c64, one], axis=1), jnp.concatenate([s64, zero], axis=1))


def kernel(x, c, ctx, c_ctx, w_ada, b_ada, attn_norm_w, w_in, q_norm_w, kv_norm_w, w_uq, w_ukv,
           lambda_q1, lambda_k1, lambda_q2, lambda_k2, subln_w, w_out, ffn_norm_w,
           router_w, router_b, w1, b1, w2, b2, final_norm_w):
    B, n, d = x.shape
    T = B * n
    depth = w_ada.shape[0]
    assert depth == 1 and d == D_MODEL and n % PRE_TM == 0 and B + 1 <= 8
    l = 0
    lam_init = 0.8 - 0.6 * math.exp(-0.3 * l)
    lam = (jnp.exp(jnp.sum(lambda_q1[l] * lambda_k1[l])) - jnp.exp(jnp.sum(lambda_q2[l] * lambda_k2[l]))
           + lam_init).reshape(1).astype(F32)

    cc = jnp.concatenate([c, c_ctx[None, :], jnp.zeros((8 - B - 1, d), F32)], axis=0)
    mod = _ada(cc, w_ada[l], b_ada[l][None, :])
    sa, sca, ga, sf, scf, gf = [mod[:B, i * d:(i + 1) * d][:, None, :] for i in range(6)]
    csa, csca = [mod[B:B + 1, i * d:(i + 1) * d][:, None, :] for i in range(2)]

    wts = _prep_weights(w_in[l], q_norm_w[l], kv_norm_w[l], w_uq[l], w_ukv[l])
    anw = attn_norm_w[l][None, :]
    dk, dvT, mk, mvT, dqT, mqT = _pre(x, sa, sca, anw, wts, _rope_tables(n), PRE_TM)
    dkc, dvTc, mkc, mvTc = _pre(ctx, csa, csca, anw, wts, None, ctx.shape[1])

    o_diff = _attention(True, lam, lam_init, dqT, dk, dvT, dkc, dvTc, subln_w[l][:, None], ATT_TQ_DIFF)
    o_mla = _attention(False, None, lam_init, mqT, mk, mvT, mkc, mvTc, None, ATT_TQ_MLA)

    x1, hf, topi, rank, gates, cnt = _post(
        o_diff.reshape(T, GROUP_WIDTH), o_mla.reshape(T, GROUP_WIDTH), x.reshape(T, d), ga,
        w_out[l].astype(BF16), ffn_norm_w[l][None, :], sf, scf,
        router_w[l].T, router_b[l][:, None], n, POST_TM)

    counts = cnt[:, 0].astype(I32)
    padded = ((counts + EXP_TM - 1) // EXP_TM) * EXP_TM
    ends = jnp.cumsum(padded)
    pad_start = ends - padded
    n_tiles = (T * TOP_K) // EXP_TM + N_EXPERTS
    e_ids = jnp.arange(N_EXPERTS, dtype=I32)
    dest = jnp.sum(jnp.where(topi[:, :, None] == e_ids, pad_start, 0), axis=-1).astype(I32) + rank
    tile_start = jnp.arange(n_tiles, dtype=I32) * EXP_TM
    tile_e = jnp.minimum(jnp.sum((ends[None, :] <= tile_start[:, None]).astype(I32), axis=1), N_EXPERTS - 1)
    n_used = (ends[-1:] // EXP_TM).astype(I32)

    def tiles(tm):
        return dest.reshape(TOP_K, T // tm, tm).transpose(1, 0, 2).reshape(T // tm, 1, TOP_K * tm)

    xs = _dispatch(tiles(DISP_TM), hf, n_tiles * EXP_TM, DISP_TM)
    ys = _experts(tile_e, n_used, xs, w1[l].astype(BF16), b1[l][:, None, :],
                  w2[l].astype(BF16), b2[l][:, None, :], EXP_TM)
    out = _combine(tiles(COMB_TM), x1, gates, gf, final_norm_w[None, :], ys, n, COMB_TM)
    return out.reshape(B, n, d)
```

```python
import functools
import math

import jax
import jax.numpy as jnp
import numpy as np
from jax import lax
from jax.experimental import pallas as pl
from jax.experimental.pallas import tpu as pltpu

F32 = jnp.float32
BF16 = jnp.bfloat16
I32 = jnp.int32
U32 = jnp.uint32
HIGH_HALF = np.uint32(0xFFFF0000)

D_MODEL = 1024
GRID_W = 64
EPS = 1e-6
ROPE_THETA = 10000.0

HEADS = 4
DIFF_HEAD_DIM = 64
HEAD_V = 128
DIFF_SCALE = DIFF_HEAD_DIM ** -0.5
MLA_NOPE = 128
MLA_ROPE = 64
MLA_QK = 256
MLA_Q_RANK = 256
MLA_KV_RANK = 128
MLA_SCALE = (MLA_NOPE + MLA_ROPE) ** -0.5
LOG2E = math.log2(math.e)
ROPE_PAIRS = 16
GROUP_WIDTH = HEADS * HEAD_V

N_EXPERTS = 32
TOP_K = 4
D_FF = 1024
SWIGLU_LIMIT = 7.0
SWIGLU_ALPHA = 1.702

LANES = 128
VMEM_LIMIT = 56 * 1024 * 1024

PRE_TM = 512
ATT_TQ_DIFF = 256
ATT_TQ_MLA = 512
ATT_CHUNKS_PER_ITER = 2
POST_TM = 512
DISP_TM = 512
EXP_TM = 512
COMB_TM = 256


def _cparams(sem, **kw):
    return pltpu.CompilerParams(dimension_semantics=sem, vmem_limit_bytes=VMEM_LIMIT, **kw)


def _rms(x, w):
    return x * lax.rsqrt(jnp.mean(x * x, axis=-1, keepdims=True) + EPS) * w


def _dot(a, b):
    return jnp.dot(a, b, preferred_element_type=F32)


def _dot_nt(a, b):
    return lax.dot_general(a, b, (((1,), (1,)), ((), ())), preferred_element_type=F32)


def _split_bf16(x):
    hi = x.astype(BF16)
    lo = (x - hi.astype(F32)).astype(BF16)
    return hi, lo


def _pack_rows(x):
    w = x.shape[1] // 2
    bits = lax.bitcast_convert_type(x.astype(BF16).astype(F32), U32)
    return (bits[:, :w] >> 16) | (bits[:, w:] & HIGH_HALF)


def _unpack_rows(u):
    return lax.bitcast_convert_type(u << 16, F32), lax.bitcast_convert_type(u & HIGH_HALF, F32)


def _ada_kernel(c_ref, w_ref, b_ref, o_ref):
    c = c_ref[...]
    s = c / (1.0 + jnp.exp(-c))
    s_hi, s_lo = _split_bf16(s)
    w_hi, w_lo = _split_bf16(w_ref[...])
    o_ref[...] = _dot(s_hi, w_hi) + _dot(s_lo, w_hi) + _dot(s_hi, w_lo) + b_ref[...]


def _ada(cc, w, b):
    rows, d = cc.shape
    n_out = w.shape[1]
    tn = 1536
    return pl.pallas_call(
        _ada_kernel,
        out_shape=jax.ShapeDtypeStruct((rows, n_out), F32),
        grid=(n_out // tn,),
        in_specs=[pl.BlockSpec((rows, d), lambda j: (0, 0)),
                  pl.BlockSpec((d, tn), lambda j: (0, j)),
                  pl.BlockSpec((1, tn), lambda j: (0, j))],
        out_specs=pl.BlockSpec((rows, tn), lambda j: (0, j)),
        compiler_params=_cparams(("arbitrary",)),
        name="adaln",
    )(cc, w, b)


C_DK, C_DV, C_CKV, C_KR = 0, 512, 1024, 1152
C_DQ, C_CQ, C_DKS, C_KRS, C_DQS = 1280, 1792, 2048, 2560, 2688
N_COLS_CTX = 1280
N_COLS = 3200


def _pre_kernel(with_q, *refs):
    if with_q:
        (x_ref, sh_ref, sc_ref, nw_ref, w_ref, qnw_ref, kvnw_ref, wuq_ref, wuqs_ref, wukv_ref,
         cd_ref, sd_ref, ck_ref, sk_ref,
         dk_ref, dvT_ref, mk_ref, mvT_ref, dqT_ref, mqT_ref) = refs
    else:
        (x_ref, sh_ref, sc_ref, nw_ref, w_ref, kvnw_ref, wukv_ref,
         dk_ref, dvT_ref, mk_ref, mvT_ref) = refs

    x = x_ref[0]
    h = _rms(x, nw_ref[...]) * (1.0 + sc_ref[0]) + sh_ref[0]
    hb = h.astype(BF16)

    def proj(c0, width):
        return _dot(hb, w_ref[:, c0:c0 + width])

    def rope(v, vs, c, s):
        return v * c + vs * s if with_q else v

    if with_q:
        cd, sd, ck, sk = cd_ref[...], sd_ref[...], ck_ref[...], sk_ref[...]
    else:
        cd = sd = ck = sk = None

    dk = proj(C_DK, GROUP_WIDTH)
    dks = proj(C_DKS, GROUP_WIDTH) if with_q else None
    dv = proj(C_DV, GROUP_WIDTH)
    for hd in range(HEADS):
        sl = slice(hd * HEAD_V, (hd + 1) * HEAD_V)
        dk_ref[0, hd] = rope(dk[:, sl], dks[:, sl] if with_q else None, cd, sd).astype(BF16)
        dvT_ref[0, hd, 0] = dv[:, sl].T.astype(BF16)

    ckv = _rms(proj(C_CKV, MLA_KV_RANK), kvnw_ref[...]).astype(BF16)
    kv = _dot(ckv, wukv_ref[...])
    kr = rope(proj(C_KR, LANES), proj(C_KRS, LANES) if with_q else None, ck, sk).astype(BF16)
    for hd in range(HEADS):
        c0 = hd * 2 * HEAD_V
        mk_ref[0, hd, :, 0:MLA_NOPE] = kv[:, c0:c0 + MLA_NOPE].astype(BF16)
        mk_ref[0, hd, :, MLA_NOPE:MLA_QK] = kr
        mvT_ref[0, hd, 0] = kv[:, c0 + MLA_NOPE:c0 + 2 * HEAD_V].T.astype(BF16)

    if with_q:
        dq = proj(C_DQ, GROUP_WIDTH)
        dqs = proj(C_DQS, GROUP_WIDTH)
        for hd in range(HEADS):
            sl = slice(hd * HEAD_V, (hd + 1) * HEAD_V)
            dqT_ref[0, hd] = ((dq[:, sl] * cd + dqs[:, sl] * sd) * (DIFF_SCALE * LOG2E)).T.astype(BF16)
        cq = _rms(proj(C_CQ, MLA_Q_RANK), qnw_ref[...]).astype(BF16)
        q = _dot(cq, wuq_ref[...])
        qs = _dot(cq, wuqs_ref[...])
        for hd in range(HEADS):
            c0 = hd * MLA_QK
            lo = q[:, c0:c0 + MLA_NOPE]
            hi = q[:, c0 + MLA_NOPE:c0 + MLA_QK] * ck + qs[:, hd * LANES:(hd + 1) * LANES] * sk
            mqT_ref[0, hd, 0:MLA_NOPE, :] = (lo * (MLA_SCALE * LOG2E)).T.astype(BF16)
            mqT_ref[0, hd, MLA_NOPE:MLA_QK, :] = (hi * (MLA_SCALE * LOG2E)).T.astype(BF16)


def _pre(x, shift, scale, nw, wts, tables, tm):
    B, n, d = x.shape
    with_q = tables is not None
    nt = n // tm
    per_b = shift.shape[0] > 1
    mod_map = (lambda b, j: (b, 0, 0)) if per_b else (lambda b, j: (0, 0, 0))
    full = lambda shape: pl.BlockSpec(shape, lambda b, j: (0,) * len(shape))
    ncols = N_COLS if with_q else N_COLS_CTX
    w_all = wts["w_all"] if with_q else wts["w_all"][:, :ncols]

    in_specs = [pl.BlockSpec((1, tm, d), lambda b, j: (b, j, 0)),
                pl.BlockSpec((1, 1, d), mod_map), pl.BlockSpec((1, 1, d), mod_map),
                full((1, d)), full((d, ncols))]
    args = [x, shift, scale, nw, w_all]
    if with_q:
        in_specs += [full((1, MLA_Q_RANK)), full((1, MLA_KV_RANK)),
                     full((MLA_Q_RANK, HEADS * MLA_QK)), full((MLA_Q_RANK, HEADS * LANES)),
                     full((MLA_KV_RANK, HEADS * 2 * HEAD_V))]
        args += [wts["qnw"], wts["kvnw"], wts["wuq"], wts["wuqs"], wts["wukv"]]
        in_specs += [pl.BlockSpec((tm, LANES), lambda b, j: (j, 0))] * 4
        args += list(tables)
    else:
        in_specs += [full((1, MLA_KV_RANK)), full((MLA_KV_RANK, HEADS * 2 * HEAD_V))]
        args += [wts["kvnw"], wts["wukv"]]

    row_map = lambda b, j: (b, 0, j, 0)
    chunk_map = lambda b, j: (b, 0, j, 0, 0)
    out_shape = [jax.ShapeDtypeStruct((B, HEADS, n, HEAD_V), BF16),
                 jax.ShapeDtypeStruct((B, HEADS, nt, HEAD_V, tm), BF16),
                 jax.ShapeDtypeStruct((B, HEADS, n, MLA_QK), BF16),
                 jax.ShapeDtypeStruct((B, HEADS, nt, HEAD_V, tm), BF16)]
    out_specs = [pl.BlockSpec((1, HEADS, tm, HEAD_V), row_map),
                 pl.BlockSpec((1, HEADS, 1, HEAD_V, tm), chunk_map),
                 pl.BlockSpec((1, HEADS, tm, MLA_QK), row_map),
                 pl.BlockSpec((1, HEADS, 1, HEAD_V, tm), chunk_map)]
    if with_q:
        col_map = lambda b, j: (b, 0, 0, j)
        out_shape += [jax.ShapeDtypeStruct((B, HEADS, HEAD_V, n), BF16),
                      jax.ShapeDtypeStruct((B, HEADS, MLA_QK, n), BF16)]
        out_specs += [pl.BlockSpec((1, HEADS, HEAD_V, tm), col_map),
                      pl.BlockSpec((1, HEADS, MLA_QK, tm), col_map)]
    return pl.pallas_call(
        functools.partial(_pre_kernel, with_q),
        out_shape=out_shape,
        grid=(B, nt),
        in_specs=in_specs,
        out_specs=out_specs,
        compiler_params=_cparams(("arbitrary", "arbitrary")),
        name="pre_attn_latent" if with_q else "pre_attn_ctx",
    )(*args)


def _attn_kernel(diff, lam_init, n_chunks, tk, cpi, *refs):
    if diff:
        (lam_ref, qT_ref, k_ref, vT_ref, kc_ref, vTc_ref, subw_ref, o_ref, s_ref, m_ref, l_ref, acc_ref) = refs
    else:
        (qT_ref, k_ref, vT_ref, kc_ref, vTc_ref, o_ref, s_ref, m_ref, l_ref, acc_ref) = refs
    qT = qT_ref[0, 0]
    tq = qT.shape[1]
    if diff:
        row = lax.broadcasted_iota(I32, qT.shape, 0)
        zero = jnp.zeros_like(qT)
        qT = jnp.concatenate([jnp.where(row < DIFF_HEAD_DIM, qT, zero),
                              jnp.where(row >= DIFF_HEAD_DIM, qT, zero)], axis=1)

    def softmax_pv(s, vTs):
        m_prev = m_ref[...]
        m_new = jnp.maximum(m_prev, jnp.max(s, axis=0, keepdims=True))
        p = jnp.exp2(s - m_new)
        psum = jnp.sum(p, axis=0, keepdims=True)
        pb = p.astype(BF16)
        pv, r0 = None, 0
        for vT in vTs:
            part = _dot(vT, pb[r0:r0 + vT.shape[1]])
            pv = part if pv is None else pv + part
            r0 += vT.shape[1]
        alpha = jnp.exp2(m_prev - m_new)
        l_ref[...] = alpha * l_ref[...] + psum
        acc_ref[...] = alpha * acc_ref[...] + pv
        m_ref[...] = m_new

    span = cpi * tk
    n_iters = n_chunks // cpi

    def scores(j):
        return _dot(k_ref[0, 0, pl.ds(pl.multiple_of(j * span, span), span), :], qT)

    def values(j):
        return [vT_ref[0, 0, j * cpi + c] for c in range(cpi)]

    m_ref[...] = jnp.full(m_ref.shape, -jnp.inf, F32)
    l_ref[...] = jnp.zeros(l_ref.shape, F32)
    acc_ref[...] = jnp.zeros(acc_ref.shape, F32)
    s_ref[...] = scores(0)

    def body(j, carry):
        s_next = scores(j + 1)
        softmax_pv(s_ref[...], values(j))
        s_ref[...] = s_next
        return carry

    lax.fori_loop(0, n_iters - 1, body, 0)
    s_last = jnp.concatenate([s_ref[...], _dot(kc_ref[0, 0], qT)], axis=0)
    softmax_pv(s_last, values(n_iters - 1) + [vTc_ref[0, 0, 0]])

    o = acc_ref[...] / l_ref[...]
    if diff:
        o = o[:, :tq] - lam_ref[0] * o[:, tq:]
        o = o * lax.rsqrt(jnp.mean(o * o, axis=0, keepdims=True) + EPS) * subw_ref[...] * (1.0 - lam_init)
    o_ref[0] = o.T.astype(BF16)


def _attention(diff, lam, lam_init, qT, k, vT, kc, vTc, subw, tq):
    B, H, dk, n = qT.shape
    n_chunks, tk = vT.shape[2], vT.shape[4]
    tc = vTc.shape[4]
    cpi = ATT_CHUNKS_PER_ITER if n_chunks % ATT_CHUNKS_PER_ITER == 0 else 1
    cols = 2 * tq if diff else tq
    bh4 = lambda b, h, i: (b, h, 0, 0)
    bh5 = lambda b, h, i: (b, h, 0, 0, 0)
    in_specs = [pl.BlockSpec((1, 1, dk, tq), lambda b, h, i: (b, h, 0, i)),
                pl.BlockSpec((1, 1, n, dk), bh4),
                pl.BlockSpec((1, 1, n_chunks, HEAD_V, tk), bh5),
                pl.BlockSpec((1, 1, tc, dk), bh4),
                pl.BlockSpec((1, 1, 1, HEAD_V, tc), bh5)]
    args = [qT, k, vT, kc, vTc]
    if diff:
        in_specs = [pl.BlockSpec(memory_space=pltpu.SMEM)] + in_specs + [
            pl.BlockSpec((HEAD_V, 1), lambda b, h, i: (0, 0))]
        args = [lam] + args + [subw]
    return pl.pallas_call(
        functools.partial(_attn_kernel, diff, lam_init, n_chunks, tk, cpi),
        out_shape=jax.ShapeDtypeStruct((B, n, H * HEAD_V), BF16),
        grid=(B, H, n // tq),
        in_specs=in_specs,
        out_specs=pl.BlockSpec((1, tq, HEAD_V), lambda b, h, i: (b, i, h)),
        scratch_shapes=[pltpu.VMEM((cpi * tk, cols), F32), pltpu.VMEM((1, cols), F32), pltpu.VMEM((1, cols), F32),
                        pltpu.VMEM((HEAD_V, cols), F32)],
        compiler_params=_cparams(("arbitrary", "arbitrary", "arbitrary")),
        name="attn_diff" if diff else "attn_mla",
    )(*args)


def _post_kernel(od_ref, om_ref, x_ref, ga_ref, wout_ref, fw_ref, sf_ref, scf_ref, rwT_ref, rb_ref,
                 x1_ref, hf_ref, topi_ref, rank_ref, gate_ref, cnt_ref, carry_ref):
    i = pl.program_id(0)

    @pl.when(i == 0)
    def _():
        carry_ref[...] = jnp.zeros(carry_ref.shape, F32)

    mix = jnp.concatenate([od_ref[...], om_ref[...]], axis=1)
    x1 = x_ref[...] + ga_ref[0] * _dot(mix, wout_ref[...])
    x1_ref[...] = x1
    hf = _rms(x1, fw_ref[...]) * (1.0 + scf_ref[0]) + sf_ref[0]
    hf_ref[...] = _pack_rows(hf)

    h_hi, h_lo = _split_bf16(hf)
    w_hi, w_lo = _split_bf16(rwT_ref[...])
    logits = _dot_nt(w_hi, h_hi) + _dot_nt(w_hi, h_lo) + _dot_nt(w_lo, h_hi) + rb_ref[...]
    n_e, tm = logits.shape
    eidx = lax.broadcasted_iota(I32, (n_e, tm), 0)

    vals, idxs = [], []
    cur = logits
    for _ in range(TOP_K):
        mx = jnp.max(cur, axis=0, keepdims=True)
        idx = jnp.min(jnp.where(cur == mx, eidx, n_e), axis=0, keepdims=True)
        vals.append(mx)
        idxs.append(idx)
        cur = jnp.where(eidx == idx, -jnp.inf, cur)
    exps = [jnp.exp(v - vals[0]) for v in vals]
    denom = exps[0] + exps[1] + exps[2] + exps[3]
    gates = jnp.concatenate([e / denom for e in exps], axis=0)

    onehot = jnp.zeros((n_e, tm), F32)
    for idx in idxs:
        onehot = onehot + jnp.where(eidx == idx, 1.0, 0.0)
    r = lax.broadcasted_iota(I32, (tm, tm), 0)
    c = lax.broadcasted_iota(I32, (tm, tm), 1)
    before = jnp.where(r < c, 1.0, 0.0).astype(BF16)
    prefix = _dot(onehot.astype(BF16), before) + carry_ref[...]
    ranks = [jnp.sum(jnp.where(eidx == idx, prefix, 0.0), axis=0, keepdims=True) for idx in idxs]
    carry = carry_ref[...] + jnp.sum(onehot, axis=1, keepdims=True)
    carry_ref[...] = carry

    topi_ref[...] = jnp.concatenate(idxs, axis=0)
    rank_ref[...] = jnp.concatenate(ranks, axis=0).astype(I32)
    gpad = jnp.concatenate([gates, jnp.zeros((LANES - TOP_K, tm), F32)], axis=0)
    gate_ref[...] = gpad.T
    cnt_ref[...] = jnp.broadcast_to(carry, cnt_ref.shape)


def _post(od, om, x, ga, wout, fw, sf, scf, rwT, rb, n_per_batch, tm):
    T, d = x.shape
    nt = T // tm
    per_b = n_per_batch // tm
    tok = lambda w: pl.BlockSpec((tm, w), lambda i: (i, 0))
    mod = pl.BlockSpec((1, 1, d), lambda i: (i // per_b, 0, 0))
    full = lambda shape: pl.BlockSpec(shape, lambda i: (0,) * len(shape))
    kt = lambda: pl.BlockSpec((TOP_K, tm), lambda i: (0, i))
    return pl.pallas_call(
        _post_kernel,
        out_shape=[jax.ShapeDtypeStruct((T, d), F32), jax.ShapeDtypeStruct((T, d // 2), U32),
                   jax.ShapeDtypeStruct((TOP_K, T), I32), jax.ShapeDtypeStruct((TOP_K, T), I32),
                   jax.ShapeDtypeStruct((T, LANES), F32),
                   jax.ShapeDtypeStruct((N_EXPERTS, LANES), F32)],
        grid=(nt,),
        in_specs=[tok(GROUP_WIDTH), tok(GROUP_WIDTH), tok(d), mod, full((d, d)), full((1, d)), mod, mod,
                  full((N_EXPERTS, d)), full((N_EXPERTS, 1))],
        out_specs=[tok(d), tok(d // 2), kt(), kt(), tok(LANES), full((N_EXPERTS, LANES))],
        scratch_shapes=[pltpu.VMEM((N_EXPERTS, 1), F32)],
        compiler_params=_cparams(("arbitrary",)),
        name="post_attn_router",
    )(od, om, x, ga, wout, fw, sf, scf, rwT, rb)


def _row_copy(src_ref, src_row, dst_ref, dst_row, sem):
    return pltpu.make_async_copy(src_ref.at[pl.ds(src_row, 1), :], dst_ref.at[pl.ds(dst_row, 1), :], sem)


def _dispatch_kernel(tm, dest_ref, hf_ref, xs_in_ref, xs_ref, sem):
    del xs_in_ref

    def issue(t, carry):
        for k in range(TOP_K):
            _row_copy(hf_ref, t, xs_ref, dest_ref[0, 0, k * tm + t], sem).start()
        return carry

    lax.fori_loop(0, tm, issue, 0)

    def drain(t, carry):
        _row_copy(hf_ref, 0, xs_ref, 0, sem).wait()
        return carry

    lax.fori_loop(0, TOP_K * tm, drain, 0)


def _dispatch(dest_tiles, hf, n_rows, tm):
    T, d = hf.shape
    xs0 = jnp.zeros((n_rows, d), hf.dtype)
    return pl.pallas_call(
        functools.partial(_dispatch_kernel, tm),
        out_shape=jax.ShapeDtypeStruct((n_rows, d), hf.dtype),
        grid=(T // tm,),
        in_specs=[pl.BlockSpec((1, 1, TOP_K * tm), lambda i: (i, 0, 0), memory_space=pltpu.SMEM),
                  pl.BlockSpec((tm, d), lambda i: (i, 0)),
                  pl.BlockSpec(memory_space=pl.ANY)],
        out_specs=pl.BlockSpec(memory_space=pl.ANY),
        scratch_shapes=[pltpu.SemaphoreType.DMA],
        input_output_aliases={2: 0},
        compiler_params=_cparams(("arbitrary",)),
        name="moe_dispatch",
    )(dest_tiles, hf, xs0)


def _expert_kernel(te_ref, nu_ref, xs_ref, w1_ref, b1_ref, w2_ref, b2_ref, ys_ref, w1b_ref, w2b_ref):
    i = pl.program_id(0)
    active = i < nu_ref[0]
    new_expert = jnp.logical_or(i == 0, te_ref[i] != te_ref[jnp.maximum(i - 1, 0)])

    @pl.when(jnp.logical_and(active, new_expert))
    def _():
        w1b_ref[...] = w1_ref[0].astype(BF16)
        w2b_ref[...] = w2_ref[0].astype(BF16)

    @pl.when(active)
    def _():
        x_lo, x_hi = _unpack_rows(xs_ref[...])
        x = jnp.concatenate([x_lo.astype(BF16), x_hi.astype(BF16)], axis=1)
        hid = _dot(x, w1b_ref[...]) + b1_ref[0]
        g = jnp.minimum(hid[:, :D_FF], SWIGLU_LIMIT)
        u = jnp.clip(hid[:, D_FF:], -SWIGLU_LIMIT, SWIGLU_LIMIT)
        act = g / (1.0 + jnp.exp(-SWIGLU_ALPHA * g)) * (u + 1.0)
        ys_ref[...] = _pack_rows(_dot(act.astype(BF16), w2b_ref[...]) + b2_ref[0])


def _experts(tile_e, n_used, xs, w1, b1, w2, b2, tm):
    n_rows, dp = xs.shape
    d = 2 * dp
    nt = n_rows // tm
    row = lambda i, te, nu: (jnp.minimum(i, nu[0] - 1), 0)
    ex = lambda i, te, nu: (te[jnp.minimum(i, nu[0] - 1)], 0, 0)
    return pl.pallas_call(
        _expert_kernel,
        out_shape=jax.ShapeDtypeStruct((n_rows, dp), U32),
        grid_spec=pltpu.PrefetchScalarGridSpec(
            num_scalar_prefetch=2,
            grid=(nt,),
            in_specs=[pl.BlockSpec((tm, dp), row),
                      pl.BlockSpec((1, d, 2 * D_FF), ex), pl.BlockSpec((1, 1, 2 * D_FF), ex),
                      pl.BlockSpec((1, D_FF, d), ex), pl.BlockSpec((1, 1, d), ex)],
            out_specs=pl.BlockSpec((tm, dp), row),
            scratch_shapes=[pltpu.VMEM((d, 2 * D_FF), BF16), pltpu.VMEM((D_FF, d), BF16)]),
        compiler_params=_cparams(("arbitrary",)),
        name="moe_experts",
    )(tile_e, n_used, xs, w1, b1, w2, b2)


def _combine_kernel(tm, dest_ref, x1_ref, gate_ref, gf_ref, fnw_ref, ys_ref, o_ref, buf_ref, sem):
    def issue(t, carry):
        for k in range(TOP_K):
            _row_copy(ys_ref, dest_ref[0, 0, k * tm + t], buf_ref.at[k], t, sem).start()
        return carry

    lax.fori_loop(0, tm, issue, 0)

    def drain(t, carry):
        _row_copy(ys_ref, 0, buf_ref.at[0], 0, sem).wait()
        return carry

    lax.fori_loop(0, TOP_K * tm, drain, 0)

    g = gate_ref[...]
    moe_lo, moe_hi = None, None
    for k in range(TOP_K):
        y_lo, y_hi = _unpack_rows(buf_ref[k])
        gk = g[:, k:k + 1]
        moe_lo = gk * y_lo if k == 0 else moe_lo + gk * y_lo
        moe_hi = gk * y_hi if k == 0 else moe_hi + gk * y_hi
    x2 = x1_ref[...] + gf_ref[0] * jnp.concatenate([moe_lo, moe_hi], axis=1)
    o_ref[...] = _rms(x2, fnw_ref[...])


def _combine(dest_tiles, x1, gates, gf, fnw, ys, n_per_batch, tm):
    T, d = x1.shape
    per_b = n_per_batch // tm
    return pl.pallas_call(
        functools.partial(_combine_kernel, tm),
        out_shape=jax.ShapeDtypeStruct((T, d), F32),
        grid=(T // tm,),
        in_specs=[pl.BlockSpec((1, 1, TOP_K * tm), lambda i: (i, 0, 0), memory_space=pltpu.SMEM),
                  pl.BlockSpec((tm, d), lambda i: (i, 0)),
                  pl.BlockSpec((tm, LANES), lambda i: (i, 0)),
                  pl.BlockSpec((1, 1, d), lambda i: (i // per_b, 0, 0)),
                  pl.BlockSpec((1, d), lambda i: (0, 0)),
                  pl.BlockSpec(memory_space=pl.ANY)],
        out_specs=pl.BlockSpec((tm, d), lambda i: (i, 0)),
        scratch_shapes=[pltpu.VMEM((TOP_K, tm, d // 2), U32), pltpu.SemaphoreType.DMA],
        compiler_params=_cparams(("arbitrary",)),
        name="moe_combine",
    )(dest_tiles, x1, gates, gf, fnw, ys)


def _rope_partner_perm(width):
    i = jnp.arange(width)
    blk = i // 64
    r = i % 64
    a, j, p = r // 32, (r // 16) % 2, r % 16
    return blk * 64 + a * 32 + (1 - j) * 16 + p


def _prep_weights(w_in, q_norm_w, kv_norm_w, w_uq, w_ukv):
    dq, dk, dv, cq, ckv, kr = jnp.split(w_in, [512, 1024, 1536, 1792, 1920], axis=1)
    d = w_in.shape[0]
    zeros64 = jnp.zeros((d, 64), F32)
    krp = jnp.concatenate([kr, zeros64], axis=1)
    perm512 = _rope_partner_perm(512)
    krs = jnp.concatenate([kr[:, _rope_partner_perm(64)], zeros64], axis=1)
    w_all = jnp.concatenate([dk, dv, ckv, krp, dq, cq, dk[:, perm512], krs, dq[:, perm512]], axis=1)
    wq = w_uq.reshape(MLA_Q_RANK, HEADS, MLA_NOPE + MLA_ROPE)
    z = jnp.zeros((MLA_Q_RANK, HEADS, 64), F32)
    wuq = jnp.concatenate([wq, z], axis=2).reshape(MLA_Q_RANK, HEADS * MLA_QK)
    wq_rope_sw = wq[:, :, MLA_NOPE:][:, :, _rope_partner_perm(64)]
    wuqs = jnp.concatenate([wq_rope_sw, z], axis=2).reshape(MLA_Q_RANK, HEADS * LANES)
    return {"w_all": w_all.astype(BF16), "qnw": q_norm_w[None, :], "kvnw": kv_norm_w[None, :],
            "wuq": wuq.astype(BF16), "wuqs": wuqs.astype(BF16), "wukv": w_ukv.astype(BF16)}


def _rope_tables(n):
    rows = n // GRID_W
    row = jnp.repeat(jnp.arange(rows, dtype=F32), GRID_W)
    col = jnp.tile(jnp.arange(GRID_W, dtype=F32), rows)
    inv = ROPE_THETA ** (-jnp.arange(ROPE_PAIRS, dtype=F32) / ROPE_PAIRS)
    ang = jnp.stack([row[:, None] * inv, col[:, None] * inv], axis=1)
    cos, sin = jnp.cos(ang), jnp.sin(ang)
    c64 = jnp.stack([cos, cos], axis=2).reshape(n, 64)
    s64 = jnp.stack([-sin, sin], axis=2).reshape(n, 64)
    one, zero = jnp.ones((n, 64), F32), jnp.zeros((n, 64), F32)
    return (jnp.concatenate([c64, c64], axis=1), jnp.concatenate([s64, s64], axis=1),
            jnp.concatenate([c64, one], axis=1), jnp.concatenate([s64, zero], axis=1))


def kernel(x, c, ctx, c_ctx, w_ada, b_ada, attn_norm_w, w_in, q_norm_w, kv_norm_w, w_uq, w_ukv,
           lambda_q1, lambda_k1, lambda_q2, lambda_k2, subln_w, w_out, ffn_norm_w,
           router_w, router_b, w1, b1, w2, b2, final_norm_w):
    B, n, d = x.shape
    T = B * n
    depth = w_ada.shape[0]
    assert depth == 1 and d == D_MODEL and n % PRE_TM == 0 and B + 1 <= 8
    l = 0
    lam_init = 0.8 - 0.6 * math.exp(-0.3 * l)
    lam = (jnp.exp(jnp.sum(lambda_q1[l] * lambda_k1[l])) - jnp.exp(jnp.sum(lambda_q2[l] * lambda_k2[l]))
           + lam_init).reshape(1).astype(F32)

    cc = jnp.concatenate([c, c_ctx[None, :], jnp.zeros((8 - B - 1, d), F32)], axis=0)
    mod = _ada(cc, w_ada[l], b_ada[l][None, :])
    sa, sca, ga, sf, scf, gf = [mod[:B, i * d:(i + 1) * d][:, None, :] for i in range(6)]
    csa, csca = [mod[B:B + 1, i * d:(i + 1) * d][:, None, :] for i in range(2)]

    wts = _prep_weights(w_in[l], q_norm_w[l], kv_norm_w[l], w_uq[l], w_ukv[l])
    anw = attn_norm_w[l][None, :]
    dk, dvT, mk, mvT, dqT, mqT = _pre(x, sa, sca, anw, wts, _rope_tables(n), PRE_TM)
    dkc, dvTc, mkc, mvTc = _pre(ctx, csa, csca, anw, wts, None, ctx.shape[1])

    o_diff = _attention(True, lam, lam_init, dqT, dk, dvT, dkc, dvTc, subln_w[l][:, None], ATT_TQ_DIFF)
    o_mla = _attention(False, None, lam_init, mqT, mk, mvT, mkc, mvTc, None, ATT_TQ_MLA)

    x1, hf, topi, rank, gates, cnt = _post(
        o_diff.reshape(T, GROUP_WIDTH), o_mla.reshape(T, GROUP_WIDTH), x.reshape(T, d), ga,
        w_out[l].astype(BF16), ffn_norm_w[l][None, :], sf, scf,
        router_w[l].T, router_b[l][:, None], n, POST_TM)

    counts = cnt[:, 0].astype(I32)
    padded = ((counts + EXP_TM - 1) // EXP_TM) * EXP_TM
    ends = jnp.cumsum(padded)
    pad_start = ends - padded
    n_tiles = (T * TOP_K) // EXP_TM + N_EXPERTS
    e_ids = jnp.arange(N_EXPERTS, dtype=I32)
    dest = jnp.sum(jnp.where(topi[:, :, None] == e_ids, pad_start, 0), axis=-1).astype(I32) + rank
    tile_start = jnp.arange(n_tiles, dtype=I32) * EXP_TM
    tile_e = jnp.minimum(jnp.sum((ends[None, :] <= tile_start[:, None]).astype(I32), axis=1), N_EXPERTS - 1)
    n_used = (ends[-1:] // EXP_TM).astype(I32)

    def tiles(tm):
        return dest.reshape(TOP_K, T // tm, tm).transpose(1, 0, 2).reshape(T // tm, 1, TOP_K * tm)

    xs = _dispatch(tiles(DISP_TM), hf, n_tiles * EXP_TM, DISP_TM)
    ys = _experts(tile_e, n_used, xs, w1[l], b1[l][:, None, :], w2[l], b2[l][:, None, :], EXP_TM)
    out = _combine(tiles(COMB_TM), x1, gates, gf, final_norm_w[None, :], ys, n, COMB_TM)
    return out.reshape(B, n, d)
```

```python
import functools
import math

import jax
import jax.numpy as jnp
import numpy as np
from jax import lax
from jax.experimental import pallas as pl
from jax.experimental.pallas import tpu as pltpu

F32 = jnp.float32
BF16 = jnp.bfloat16
I32 = jnp.int32
U32 = jnp.uint32
HIGH_HALF = np.uint32(0xFFFF0000)

D_MODEL = 1024
GRID_W = 64
EPS = 1e-6
ROPE_THETA = 10000.0

HEADS = 4
DIFF_HEAD_DIM = 64
HEAD_V = 128
DIFF_SCALE = DIFF_HEAD_DIM ** -0.5
MLA_NOPE = 128
MLA_ROPE = 64
MLA_QK = 256
MLA_Q_RANK = 256
MLA_KV_RANK = 128
MLA_SCALE = (MLA_NOPE + MLA_ROPE) ** -0.5
LOG2E = math.log2(math.e)
ROPE_PAIRS = 16
GROUP_WIDTH = HEADS * HEAD_V

N_EXPERTS = 32
TOP_K = 4
D_FF = 1024
SWIGLU_LIMIT = 7.0
SWIGLU_ALPHA = 1.702

LANES = 128
DMA_THREADS = 2
VMEM_LIMIT = 56 * 1024 * 1024

PRE_TM = 512
ATT_TQ_DIFF = 256
ATT_TQ_MLA = 512
ATT_CHUNKS_PER_ITER = 2
TOK_TM = 512
EXP_TM = 512
SEG_ALIGN = 8
SEG_MAX = TOK_TM
SEG_BLOCKS = tuple(SEG_ALIGN << b for b in reversed(range((SEG_MAX // SEG_ALIGN).bit_length())))


def _local_rows(tm):
    rows = TOP_K * tm + N_EXPERTS * (SEG_ALIGN - 1)
    return -(-rows // LANES) * LANES


def _cparams(sem, **kw):
    return pltpu.CompilerParams(dimension_semantics=sem, vmem_limit_bytes=VMEM_LIMIT, **kw)


def _rms(x, w):
    return x * lax.rsqrt(jnp.mean(x * x, axis=-1, keepdims=True) + EPS) * w


def _dot(a, b):
    return jnp.dot(a, b, preferred_element_type=F32)


def _dot_nt(a, b):
    return lax.dot_general(a, b, (((1,), (1,)), ((), ())), preferred_element_type=F32)


def _split_bf16(x):
    hi = x.astype(BF16)
    lo = (x - hi.astype(F32)).astype(BF16)
    return hi, lo


def _pack_rows(x):
    w = x.shape[1] // 2
    bits = lax.bitcast_convert_type(x.astype(BF16).astype(F32), U32)
    return (bits[:, :w] >> 16) | (bits[:, w:] & HIGH_HALF)


def _unpack_rows(u):
    return lax.bitcast_convert_type(u << 16, F32), lax.bitcast_convert_type(u & HIGH_HALF, F32)


def _ada_kernel(c_ref, w_ref, b_ref, o_ref):
    c = c_ref[...]
    s = c / (1.0 + jnp.exp(-c))
    s_hi, s_lo = _split_bf16(s)
    w_hi, w_lo = _split_bf16(w_ref[...])
    o_ref[...] = _dot(s_hi, w_hi) + _dot(s_lo, w_hi) + _dot(s_hi, w_lo) + b_ref[...]


def _ada(cc, w, b):
    rows, d = cc.shape
    n_out = w.shape[1]
    tn = 1536
    return pl.pallas_call(
        _ada_kernel,
        out_shape=jax.ShapeDtypeStruct((rows, n_out), F32),
        grid=(n_out // tn,),
        in_specs=[pl.BlockSpec((rows, d), lambda j: (0, 0)),
                  pl.BlockSpec((d, tn), lambda j: (0, j)),
                  pl.BlockSpec((1, tn), lambda j: (0, j))],
        out_specs=pl.BlockSpec((rows, tn), lambda j: (0, j)),
        compiler_params=_cparams(("arbitrary",)),
        name="adaln",
    )(cc, w, b)


C_DK, C_DV, C_CKV, C_KR = 0, 512, 1024, 1152
C_DQ, C_CQ, C_DKS, C_KRS, C_DQS = 1280, 1792, 2048, 2560, 2688
N_COLS_CTX = 1280
N_COLS = 3200


def _pre_kernel(with_q, *refs):
    if with_q:
        (x_ref, sh_ref, sc_ref, nw_ref, w_ref, qnw_ref, kvnw_ref, wuq_ref, wuqs_ref, wukv_ref,
         cd_ref, sd_ref, ck_ref, sk_ref,
         dk_ref, dvT_ref, mk_ref, mvT_ref, dqT_ref, mqT_ref) = refs
    else:
        (x_ref, sh_ref, sc_ref, nw_ref, w_ref, kvnw_ref, wukv_ref,
         dk_ref, dvT_ref, mk_ref, mvT_ref) = refs

    x = x_ref[0]
    h = _rms(x, nw_ref[...]) * (1.0 + sc_ref[0]) + sh_ref[0]
    hb = h.astype(BF16)

    def proj(c0, width):
        return _dot(hb, w_ref[:, c0:c0 + width])

    def rope(v, vs, c, s):
        return v * c + vs * s if with_q else v

    if with_q:
        cd, sd, ck, sk = cd_ref[...], sd_ref[...], ck_ref[...], sk_ref[...]
    else:
        cd = sd = ck = sk = None

    dk = proj(C_DK, GROUP_WIDTH)
    dks = proj(C_DKS, GROUP_WIDTH) if with_q else None
    dv = proj(C_DV, GROUP_WIDTH)
    for hd in range(HEADS):
        sl = slice(hd * HEAD_V, (hd + 1) * HEAD_V)
        dk_ref[0, hd] = rope(dk[:, sl], dks[:, sl] if with_q else None, cd, sd).astype(BF16)
        dvT_ref[0, hd, 0] = dv[:, sl].T.astype(BF16)

    ckv = _rms(proj(C_CKV, MLA_KV_RANK), kvnw_ref[...]).astype(BF16)
    kv = _dot(ckv, wukv_ref[...])
    kr = rope(proj(C_KR, LANES), proj(C_KRS, LANES) if with_q else None, ck, sk).astype(BF16)
    for hd in range(HEADS):
        c0 = hd * 2 * HEAD_V
        mk_ref[0, hd, :, 0:MLA_NOPE] = kv[:, c0:c0 + MLA_NOPE].astype(BF16)
        mk_ref[0, hd, :, MLA_NOPE:MLA_QK] = kr
        mvT_ref[0, hd, 0] = kv[:, c0 + MLA_NOPE:c0 + 2 * HEAD_V].T.astype(BF16)

    if with_q:
        dq = proj(C_DQ, GROUP_WIDTH)
        dqs = proj(C_DQS, GROUP_WIDTH)
        for hd in range(HEADS):
            sl = slice(hd * HEAD_V, (hd + 1) * HEAD_V)
            dqT_ref[0, hd] = ((dq[:, sl] * cd + dqs[:, sl] * sd) * (DIFF_SCALE * LOG2E)).T.astype(BF16)
        cq = _rms(proj(C_CQ, MLA_Q_RANK), qnw_ref[...]).astype(BF16)
        q = _dot(cq, wuq_ref[...])
        qs = _dot(cq, wuqs_ref[...])
        for hd in range(HEADS):
            c0 = hd * MLA_QK
            lo = q[:, c0:c0 + MLA_NOPE]
            hi = q[:, c0 + MLA_NOPE:c0 + MLA_QK] * ck + qs[:, hd * LANES:(hd + 1) * LANES] * sk
            mqT_ref[0, hd, 0:MLA_NOPE, :] = (lo * (MLA_SCALE * LOG2E)).T.astype(BF16)
            mqT_ref[0, hd, MLA_NOPE:MLA_QK, :] = (hi * (MLA_SCALE * LOG2E)).T.astype(BF16)


def _pre(x, shift, scale, nw, wts, tables, tm):
    B, n, d = x.shape
    with_q = tables is not None
    nt = n // tm
    per_b = shift.shape[0] > 1
    mod_map = (lambda b, j: (b, 0, 0)) if per_b else (lambda b, j: (0, 0, 0))
    full = lambda shape: pl.BlockSpec(shape, lambda b, j: (0,) * len(shape))
    ncols = N_COLS if with_q else N_COLS_CTX
    w_all = wts["w_all"] if with_q else wts["w_all"][:, :ncols]

    in_specs = [pl.BlockSpec((1, tm, d), lambda b, j: (b, j, 0)),
                pl.BlockSpec((1, 1, d), mod_map), pl.BlockSpec((1, 1, d), mod_map),
                full((1, d)), full((d, ncols))]
    args = [x, shift, scale, nw, w_all]
    if with_q:
        in_specs += [full((1, MLA_Q_RANK)), full((1, MLA_KV_RANK)),
                     full((MLA_Q_RANK, HEADS * MLA_QK)), full((MLA_Q_RANK, HEADS * LANES)),
                     full((MLA_KV_RANK, HEADS * 2 * HEAD_V))]
        args += [wts["qnw"], wts["kvnw"], wts["wuq"], wts["wuqs"], wts["wukv"]]
        in_specs += [pl.BlockSpec((tm, LANES), lambda b, j: (j, 0))] * 4
        args += list(tables)
    else:
        in_specs += [full((1, MLA_KV_RANK)), full((MLA_KV_RANK, HEADS * 2 * HEAD_V))]
        args += [wts["kvnw"], wts["wukv"]]

    row_map = lambda b, j: (b, 0, j, 0)
    chunk_map = lambda b, j: (b, 0, j, 0, 0)
    out_shape = [jax.ShapeDtypeStruct((B, HEADS, n, HEAD_V), BF16),
                 jax.ShapeDtypeStruct((B, HEADS, nt, HEAD_V, tm), BF16),
                 jax.ShapeDtypeStruct((B, HEADS, n, MLA_QK), BF16),
                 jax.ShapeDtypeStruct((B, HEADS, nt, HEAD_V, tm), BF16)]
    out_specs = [pl.BlockSpec((1, HEADS, tm, HEAD_V), row_map),
                 pl.BlockSpec((1, HEADS, 1, HEAD_V, tm), chunk_map),
                 pl.BlockSpec((1, HEADS, tm, MLA_QK), row_map),
                 pl.BlockSpec((1, HEADS, 1, HEAD_V, tm), chunk_map)]
    if with_q:
        col_map = lambda b, j: (b, 0, 0, j)
        out_shape += [jax.ShapeDtypeStruct((B, HEADS, HEAD_V, n), BF16),
                      jax.ShapeDtypeStruct((B, HEADS, MLA_QK, n), BF16)]
        out_specs += [pl.BlockSpec((1, HEADS, HEAD_V, tm), col_map),
                      pl.BlockSpec((1, HEADS, MLA_QK, tm), col_map)]
    return pl.pallas_call(
        functools.partial(_pre_kernel, with_q),
        out_shape=out_shape,
        grid=(B, nt),
        in_specs=in_specs,
        out_specs=out_specs,
        compiler_params=_cparams(("arbitrary", "arbitrary")),
        name="pre_attn_latent" if with_q else "pre_attn_ctx",
    )(*args)


def _attn_kernel(diff, lam_init, n_chunks, tk, cpi, *refs):
    if diff:
        (lam_ref, qT_ref, k_ref, vT_ref, kc_ref, vTc_ref, subw_ref, o_ref, s_ref, m_ref, l_ref, acc_ref) = refs
    else:
        (qT_ref, k_ref, vT_ref, kc_ref, vTc_ref, o_ref, s_ref, m_ref, l_ref, acc_ref) = refs
    qT = qT_ref[0, 0]
    tq = qT.shape[1]
    if diff:
        row = lax.broadcasted_iota(I32, qT.shape, 0)
        zero = jnp.zeros_like(qT)
        qT = jnp.concatenate([jnp.where(row < DIFF_HEAD_DIM, qT, zero),
                              jnp.where(row >= DIFF_HEAD_DIM, qT, zero)], axis=1)

    def softmax_pv(s, vTs):
        m_prev = m_ref[...]
        m_new = jnp.maximum(m_prev, jnp.max(s, axis=0, keepdims=True))
        p = jnp.exp2(s - m_new)
        psum = jnp.sum(p, axis=0, keepdims=True)
        pb = p.astype(BF16)
        pv, r0 = None, 0
        for vT in vTs:
            part = _dot(vT, pb[r0:r0 + vT.shape[1]])
            pv = part if pv is None else pv + part
            r0 += vT.shape[1]
        alpha = jnp.exp2(m_prev - m_new)
        l_ref[...] = alpha * l_ref[...] + psum
        acc_ref[...] = alpha * acc_ref[...] + pv
        m_ref[...] = m_new

    span = cpi * tk
    n_iters = n_chunks // cpi

    def scores(j):
        return _dot(k_ref[0, 0, pl.ds(pl.multiple_of(j * span, span), span), :], qT)

    def values(j):
        return [vT_ref[0, 0, j * cpi + c] for c in range(cpi)]

    m_ref[...] = jnp.full(m_ref.shape, -jnp.inf, F32)
    l_ref[...] = jnp.zeros(l_ref.shape, F32)
    acc_ref[...] = jnp.zeros(acc_ref.shape, F32)
    s_ref[0] = scores(0)

    def step(j, cur, nxt):
        s_next = scores(j + 1)
        softmax_pv(s_ref[cur], values(j))
        s_ref[nxt] = s_next

    def body(jj, carry):
        step(2 * jj, 0, 1)
        step(2 * jj + 1, 1, 0)
        return carry

    lax.fori_loop(0, (n_iters - 1) // 2, body, 0)
    last = 0
    if (n_iters - 1) % 2:
        step(n_iters - 2, 0, 1)
        last = 1
    s_last = jnp.concatenate([s_ref[last], _dot(kc_ref[0, 0], qT)], axis=0)
    softmax_pv(s_last, values(n_iters - 1) + [vTc_ref[0, 0, 0]])

    o = acc_ref[...] / l_ref[...]
    if diff:
        o = o[:, :tq] - lam_ref[0] * o[:, tq:]
        o = o * lax.rsqrt(jnp.mean(o * o, axis=0, keepdims=True) + EPS) * subw_ref[...] * (1.0 - lam_init)
    o_ref[0] = o.T.astype(BF16)


def _attention(diff, lam, lam_init, qT, k, vT, kc, vTc, subw, tq):
    B, H, dk, n = qT.shape
    n_chunks, tk = vT.shape[2], vT.shape[4]
    tc = vTc.shape[4]
    cpi = ATT_CHUNKS_PER_ITER if n_chunks % ATT_CHUNKS_PER_ITER == 0 else 1
    cols = 2 * tq if diff else tq
    bh4 = lambda b, h, i: (b, h, 0, 0)
    bh5 = lambda b, h, i: (b, h, 0, 0, 0)
    in_specs = [pl.BlockSpec((1, 1, dk, tq), lambda b, h, i: (b, h, 0, i)),
                pl.BlockSpec((1, 1, n, dk), bh4),
                pl.BlockSpec((1, 1, n_chunks, HEAD_V, tk), bh5),
                pl.BlockSpec((1, 1, tc, dk), bh4),
                pl.BlockSpec((1, 1, 1, HEAD_V, tc), bh5)]
    args = [qT, k, vT, kc, vTc]
    if diff:
        in_specs = [pl.BlockSpec(memory_space=pltpu.SMEM)] + in_specs + [
            pl.BlockSpec((HEAD_V, 1), lambda b, h, i: (0, 0))]
        args = [lam] + args + [subw]
    return pl.pallas_call(
        functools.partial(_attn_kernel, diff, lam_init, n_chunks, tk, cpi),
        out_shape=jax.ShapeDtypeStruct((B, n, H * HEAD_V), BF16),
        grid=(B, H, n // tq),
        in_specs=in_specs,
        out_specs=pl.BlockSpec((1, tq, HEAD_V), lambda b, h, i: (b, i, h)),
        scratch_shapes=[pltpu.VMEM((2, cpi * tk, cols), F32), pltpu.VMEM((1, cols), F32), pltpu.VMEM((1, cols), F32),
                        pltpu.VMEM((HEAD_V, cols), F32)],
        compiler_params=_cparams(("arbitrary", "arbitrary", "arbitrary")),
        name="attn_diff" if diff else "attn_mla",
    )(*args)


def _post_kernel(od_ref, om_ref, x_ref, ga_ref, wout_ref, fw_ref, sf_ref, scf_ref, rwT_ref, rb_ref,
                 x1_ref, hf_ref, lpos_ref, meta_ref, seg_ref, carry_ref):
    i = pl.program_id(0)

    @pl.when(i == 0)
    def _():
        carry_ref[...] = jnp.zeros(carry_ref.shape, F32)

    mix = jnp.concatenate([od_ref[...], om_ref[...]], axis=1)
    x1 = x_ref[...] + ga_ref[0] * _dot(mix, wout_ref[...])
    x1_ref[...] = x1
    hf = _rms(x1, fw_ref[...]) * (1.0 + scf_ref[0]) + sf_ref[0]
    hf_ref[...] = hf.astype(BF16)

    h_hi, h_lo = _split_bf16(hf)
    w_hi, w_lo = _split_bf16(rwT_ref[...])
    logits = _dot_nt(w_hi, h_hi) + _dot_nt(w_hi, h_lo) + _dot_nt(w_lo, h_hi) + rb_ref[...]
    n_e, tm = logits.shape
    eidx = lax.broadcasted_iota(I32, (n_e, tm), 0)

    vals, idxs = [], []
    cur = logits
    for _ in range(TOP_K):
        mx = jnp.max(cur, axis=0, keepdims=True)
        idx = jnp.min(jnp.where(cur == mx, eidx, n_e), axis=0, keepdims=True)
        vals.append(mx)
        idxs.append(idx)
        cur = jnp.where(eidx == idx, -jnp.inf, cur)
    exps = [jnp.exp(v - vals[0]) for v in vals]
    denom = exps[0] + exps[1] + exps[2] + exps[3]
    gates = jnp.concatenate([e / denom for e in exps], axis=0)

    onehot = jnp.zeros((n_e, tm), F32)
    for idx in idxs:
        onehot = onehot + jnp.where(eidx == idx, 1.0, 0.0)
    r = lax.broadcasted_iota(I32, (tm, tm), 0)
    c = lax.broadcasted_iota(I32, (tm, tm), 1)
    before = jnp.where(r < c, 1.0, 0.0).astype(BF16)
    prefix = _dot(onehot.astype(BF16), before)
    cnt8 = jnp.ceil(jnp.sum(onehot, axis=1, keepdims=True) * (1.0 / SEG_ALIGN)) * SEG_ALIGN
    er = lax.broadcasted_iota(I32, (n_e, n_e), 0)
    ec = lax.broadcasted_iota(I32, (n_e, n_e), 1)
    lower = jnp.where(ec < er, 1.0, 0.0).astype(BF16)
    lstart = _dot(lower, jnp.broadcast_to(cnt8, (n_e, LANES)).astype(BF16))[:, 0:1]
    where_in_tile = prefix + lstart
    lpos = jnp.concatenate(
        [jnp.sum(jnp.where(eidx == idx, where_in_tile, 0.0), axis=0, keepdims=True) for idx in idxs], axis=0)
    done = carry_ref[...]
    carry_ref[...] = done + cnt8

    lpos_ref[...] = lpos.astype(I32)
    meta = jnp.concatenate([gates, lpos, jnp.zeros((LANES - 2 * TOP_K, tm), F32)], axis=0)
    meta_ref[...] = meta.T
    lane = lax.broadcasted_iota(I32, (n_e, LANES), 1)
    seg_ref[0] = jnp.where(lane == 0, cnt8, jnp.where(lane == 1, lstart, jnp.where(lane == 2, done, 0.0)))


def _post(od, om, x, ga, wout, fw, sf, scf, rwT, rb, n_per_batch, tm):
    T, d = x.shape
    nt = T // tm
    per_b = n_per_batch // tm
    tok = lambda w: pl.BlockSpec((tm, w), lambda i: (i, 0))
    mod = pl.BlockSpec((1, 1, d), lambda i: (i // per_b, 0, 0))
    full = lambda shape: pl.BlockSpec(shape, lambda i: (0,) * len(shape))
    return pl.pallas_call(
        _post_kernel,
        out_shape=[jax.ShapeDtypeStruct((T, d), F32), jax.ShapeDtypeStruct((T, d), BF16),
                   jax.ShapeDtypeStruct((TOP_K, T), I32), jax.ShapeDtypeStruct((T, LANES), F32),
                   jax.ShapeDtypeStruct((nt, N_EXPERTS, LANES), F32)],
        grid=(nt,),
        in_specs=[tok(GROUP_WIDTH), tok(GROUP_WIDTH), tok(d), mod, full((d, d)), full((1, d)), mod, mod,
                  full((N_EXPERTS, d)), full((N_EXPERTS, 1))],
        out_specs=[tok(d), tok(d), pl.BlockSpec((TOP_K, tm), lambda i: (0, i)), tok(LANES),
                   pl.BlockSpec((1, N_EXPERTS, LANES), lambda i: (i, 0, 0))],
        scratch_shapes=[pltpu.VMEM((N_EXPERTS, 1), F32)],
        compiler_params=_cparams(("arbitrary",)),
        name="post_attn_router",
    )(od, om, x, ga, wout, fw, sf, scf, rwT, rb)


def _segment_copies(i, cnt_ref, ls_ref, gs_ref, local_ref, hbm_ref, sem, to_hbm, wait):
    def per_expert(e, carry):
        n, lo, go = cnt_ref[i, e], ls_ref[i, e], gs_ref[i, e]
        for b, size in enumerate(SEG_BLOCKS):
            larger = (n // (2 * size)) * (2 * size)

            @pl.when((n & size) != 0)
            def _():
                loc = local_ref.at[pl.ds(pl.multiple_of(lo + larger, SEG_ALIGN), size), :]
                glob = hbm_ref.at[pl.ds(pl.multiple_of(go + larger, SEG_ALIGN), size), :]
                cp = pltpu.make_async_copy(loc, glob, sem) if to_hbm else pltpu.make_async_copy(glob, loc, sem)
                if wait:
                    cp.wait()
                else:
                    cp.start(priority=b % DMA_THREADS)
        return carry

    lax.fori_loop(0, N_EXPERTS, per_expert, 0)


def _dispatch_kernel(cnt_ref, ls_ref, gs_ref, lpos_ref, hf_ref, xs_in_ref, xs_ref, sorted_ref, sem):
    del xs_in_ref
    i = pl.program_id(0)
    lpos = lpos_ref[...]
    rows, tm = sorted_ref.shape[0], lpos.shape[1]
    r = lax.broadcasted_iota(I32, (rows, tm), 0)
    hit = r == lpos[0:1]
    for k in range(1, TOP_K):
        hit = jnp.logical_or(hit, r == lpos[k:k + 1])
    perm = jnp.where(hit, 1.0, 0.0).astype(BF16)
    sorted_ref[...] = _pack_rows(_dot(perm, hf_ref[...]))
    _segment_copies(i, cnt_ref, ls_ref, gs_ref, sorted_ref, xs_ref, sem, True, False)
    _segment_copies(i, cnt_ref, ls_ref, gs_ref, sorted_ref, xs_ref, sem, True, True)


def _dispatch(cnt8, lstart, gstart, lpos, hf, n_rows, tm):
    T, d = hf.shape
    xs0 = jnp.zeros((n_rows, d // 2), U32)
    return pl.pallas_call(
        _dispatch_kernel,
        out_shape=jax.ShapeDtypeStruct((n_rows, d // 2), U32),
        grid_spec=pltpu.PrefetchScalarGridSpec(
            num_scalar_prefetch=3,
            grid=(T // tm,),
            in_specs=[pl.BlockSpec((TOP_K, tm), lambda i, *_: (0, i)),
                      pl.BlockSpec((tm, d), lambda i, *_: (i, 0)),
                      pl.BlockSpec(memory_space=pl.ANY)],
            out_specs=pl.BlockSpec(memory_space=pl.ANY),
            scratch_shapes=[pltpu.VMEM((_local_rows(tm), d // 2), U32), pltpu.SemaphoreType.DMA]),
        input_output_aliases={5: 0},
        compiler_params=_cparams(("arbitrary",)),
        name="moe_dispatch",
    )(cnt8, lstart, gstart, lpos, hf, xs0)


def _expert_kernel(te_ref, nu_ref, xs_ref, w1_ref, b1_ref, w2_ref, b2_ref, ys_ref, w1b_ref, w2b_ref):
    i = pl.program_id(0)
    active = i < nu_ref[0]
    new_expert = jnp.logical_or(i == 0, te_ref[i] != te_ref[jnp.maximum(i - 1, 0)])

    @pl.when(jnp.logical_and(active, new_expert))
    def _():
        w1b_ref[...] = w1_ref[0].astype(BF16)
        w2b_ref[...] = w2_ref[0].astype(BF16)

    @pl.when(active)
    def _():
        x_lo, x_hi = _unpack_rows(xs_ref[...])
        x = jnp.concatenate([x_lo.astype(BF16), x_hi.astype(BF16)], axis=1)
        hid = _dot(x, w1b_ref[...]) + b1_ref[0]
        g = jnp.minimum(hid[:, :D_FF], SWIGLU_LIMIT)
        u = jnp.clip(hid[:, D_FF:], -SWIGLU_LIMIT, SWIGLU_LIMIT)
        act = g / (1.0 + jnp.exp(-SWIGLU_ALPHA * g)) * (u + 1.0)
        ys_ref[...] = _pack_rows(_dot(act.astype(BF16), w2b_ref[...]) + b2_ref[0])


def _experts(tile_e, n_used, xs, w1, b1, w2, b2, tm):
    n_rows, dp = xs.shape
    d = 2 * dp
    nt = n_rows // tm
    row = lambda i, te, nu: (jnp.minimum(i, nu[0] - 1), 0)
    ex = lambda i, te, nu: (te[jnp.minimum(i, nu[0] - 1)], 0, 0)
    return pl.pallas_call(
        _expert_kernel,
        out_shape=jax.ShapeDtypeStruct((n_rows, dp), U32),
        grid_spec=pltpu.PrefetchScalarGridSpec(
            num_scalar_prefetch=2,
            grid=(nt,),
            in_specs=[pl.BlockSpec((tm, dp), row),
                      pl.BlockSpec((1, d, 2 * D_FF), ex), pl.BlockSpec((1, 1, 2 * D_FF), ex),
                      pl.BlockSpec((1, D_FF, d), ex), pl.BlockSpec((1, 1, d), ex)],
            out_specs=pl.BlockSpec((tm, dp), row),
            scratch_shapes=[pltpu.VMEM((d, 2 * D_FF), BF16), pltpu.VMEM((D_FF, d), BF16)]),
        compiler_params=_cparams(("arbitrary",)),
        name="moe_experts",
    )(tile_e, n_used, xs, w1, b1, w2, b2)


def _combine_kernel(cnt_ref, ls_ref, gs_ref, x1_ref, meta_ref, gf_ref, fnw_ref, ys_ref, o_ref, yl_ref, sem):
    i = pl.program_id(0)

    @pl.when(i == 0)
    def _():
        yl_ref[...] = jnp.zeros(yl_ref.shape, U32)

    _segment_copies(i, cnt_ref, ls_ref, gs_ref, yl_ref, ys_ref, sem, False, False)
    _segment_copies(i, cnt_ref, ls_ref, gs_ref, yl_ref, ys_ref, sem, False, True)

    meta = meta_ref[...]
    tm, rows = meta.shape[0], yl_ref.shape[0]
    col = lax.broadcasted_iota(I32, (tm, rows), 1).astype(F32)
    weights = jnp.zeros((tm, rows), F32)
    for k in range(TOP_K):
        weights = weights + jnp.where(col == meta[:, TOP_K + k:TOP_K + k + 1], meta[:, k:k + 1], 0.0)
    weights = weights.astype(BF16)
    y_lo, y_hi = _unpack_rows(yl_ref[...])
    moe = jnp.concatenate([_dot(weights, y_lo.astype(BF16)), _dot(weights, y_hi.astype(BF16))], axis=1)
    x2 = x1_ref[...] + gf_ref[0] * moe
    o_ref[...] = _rms(x2, fnw_ref[...])


def _combine(cnt8, lstart, gstart, x1, meta, gf, fnw, ys, n_per_batch, tm):
    T, d = x1.shape
    per_b = n_per_batch // tm
    return pl.pallas_call(
        _combine_kernel,
        out_shape=jax.ShapeDtypeStruct((T, d), F32),
        grid_spec=pltpu.PrefetchScalarGridSpec(
            num_scalar_prefetch=3,
            grid=(T // tm,),
            in_specs=[pl.BlockSpec((tm, d), lambda i, *_: (i, 0)),
                      pl.BlockSpec((tm, LANES), lambda i, *_: (i, 0)),
                      pl.BlockSpec((1, 1, d), lambda i, *_: (i // per_b, 0, 0)),
                      pl.BlockSpec((1, d), lambda i, *_: (0, 0)),
                      pl.BlockSpec(memory_space=pl.ANY)],
            out_specs=pl.BlockSpec((tm, d), lambda i, *_: (i, 0)),
            scratch_shapes=[pltpu.VMEM((_local_rows(tm), d // 2), U32), pltpu.SemaphoreType.DMA]),
        compiler_params=_cparams(("arbitrary",)),
        name="moe_combine",
    )(cnt8, lstart, gstart, x1, meta, gf, fnw, ys)


def _rope_partner_perm(width):
    i = jnp.arange(width)
    blk = i // 64
    r = i % 64
    a, j, p = r // 32, (r // 16) % 2, r % 16
    return blk * 64 + a * 32 + (1 - j) * 16 + p


def _prep_weights(w_in, q_norm_w, kv_norm_w, w_uq, w_ukv):
    dq, dk, dv, cq, ckv, kr = jnp.split(w_in, [512, 1024, 1536, 1792, 1920], axis=1)
    d = w_in.shape[0]
    zeros64 = jnp.zeros((d, 64), F32)
    krp = jnp.concatenate([kr, zeros64], axis=1)
    perm512 = _rope_partner_perm(512)
    krs = jnp.concatenate([kr[:, _rope_partner_perm(64)], zeros64], axis=1)
    w_all = jnp.concatenate([dk, dv, ckv, krp, dq, cq, dk[:, perm512], krs, dq[:, perm512]], axis=1)
    wq = w_uq.reshape(MLA_Q_RANK, HEADS, MLA_NOPE + MLA_ROPE)
    z = jnp.zeros((MLA_Q_RANK, HEADS, 64), F32)
    wuq = jnp.concatenate([wq, z], axis=2).reshape(MLA_Q_RANK, HEADS * MLA_QK)
    wq_rope_sw = wq[:, :, MLA_NOPE:][:, :, _rope_partner_perm(64)]
    wuqs = jnp.concatenate([wq_rope_sw, z], axis=2).reshape(MLA_Q_RANK, HEADS * LANES)
    return {"w_all": w_all.astype(BF16), "qnw": q_norm_w[None, :], "kvnw": kv_norm_w[None, :],
            "wuq": wuq.astype(BF16), "wuqs": wuqs.astype(BF16), "wukv": w_ukv.astype(BF16)}


def _rope_tables(n):
    rows = n // GRID_W
    row = jnp.repeat(jnp.arange(rows, dtype=F32), GRID_W)
    col = jnp.tile(jnp.arange(GRID_W, dtype=F32), rows)
    inv = ROPE_THETA ** (-jnp.arange(ROPE_PAIRS, dtype=F32) / ROPE_PAIRS)
    ang = jnp.stack([row[:, None] * inv, col[:, None] * inv], axis=1)
    cos, sin = jnp.cos(ang), jnp.sin(ang)
    c64 = jnp.stack([cos, cos], axis=2).reshape(n, 64)
    s64 = jnp.stack([-sin, sin], axis=2).reshape(n, 64)
    one, zero = jnp.ones((n, 64), F32), jnp.zeros((n, 64), F32)
    return (jnp.concatenate([c64, c64], axis=1), jnp.concatenate([s64, s64], axis=1),
            jnp.concatenate([c64, one], axis=1), jnp.concatenate([s64, zero], axis=1))


def kernel(x, c, ctx, c_ctx, w_ada, b_ada, attn_norm_w, w_in, q_norm_w, kv_norm_w, w_uq, w_ukv,
           lambda_q1, lambda_k1, lambda_q2, lambda_k2, subln_w, w_out, ffn_norm_w,
           router_w, router_b, w1, b1, w2, b2, final_norm_w):
    B, n, d = x.shape
    T = B * n
    depth = w_ada.shape[0]
    assert depth == 1 and d == D_MODEL and n % PRE_TM == 0 and B + 1 <= 8
    l = 0
    lam_init = 0.8 - 0.6 * math.exp(-0.3 * l)
    lam = (jnp.exp(jnp.sum(lambda_q1[l] * lambda_k1[l])) - jnp.exp(jnp.sum(lambda_q2[l] * lambda_k2[l]))
           + lam_init).reshape(1).astype(F32)

    cc = jnp.concatenate([c, c_ctx[None, :], jnp.zeros((8 - B - 1, d), F32)], axis=0)
    mod = _ada(cc, w_ada[l], b_ada[l][None, :])
    sa, sca, ga, sf, scf, gf = [mod[:B, i * d:(i + 1) * d][:, None, :] for i in range(6)]
    csa, csca = [mod[B:B + 1, i * d:(i + 1) * d][:, None, :] for i in range(2)]

    wts = _prep_weights(w_in[l], q_norm_w[l], kv_norm_w[l], w_uq[l], w_ukv[l])
    anw = attn_norm_w[l][None, :]
    dk, dvT, mk, mvT, dqT, mqT = _pre(x, sa, sca, anw, wts, _rope_tables(n), PRE_TM)
    dkc, dvTc, mkc, mvTc = _pre(ctx, csa, csca, anw, wts, None, ctx.shape[1])

    o_diff = _attention(True, lam, lam_init, dqT, dk, dvT, dkc, dvTc, subln_w[l][:, None], ATT_TQ_DIFF)
    o_mla = _attention(False, None, lam_init, mqT, mk, mvT, mkc, mvTc, None, ATT_TQ_MLA)

    x1, hf, lpos, meta, seg = _post(
        o_diff.reshape(T, GROUP_WIDTH), o_mla.reshape(T, GROUP_WIDTH), x.reshape(T, d), ga,
        w_out[l].astype(BF16), ffn_norm_w[l][None, :], sf, scf,
        router_w[l].T, router_b[l][:, None], n, TOK_TM)

    cnt8, lstart, done = [seg[:, :, j].astype(I32) for j in range(3)]
    totals = done[-1] + cnt8[-1]
    padded = ((totals + EXP_TM - 1) // EXP_TM) * EXP_TM
    ends = jnp.cumsum(padded)
    gstart = (ends - padded)[None, :] + done
    n_tok_tiles = T // TOK_TM
    n_tiles = -(-(T * TOP_K + n_tok_tiles * N_EXPERTS * (SEG_ALIGN - 1)) // EXP_TM) + N_EXPERTS
    tile_start = jnp.arange(n_tiles, dtype=I32) * EXP_TM
    tile_e = jnp.minimum(jnp.sum((ends[None, :] <= tile_start[:, None]).astype(I32), axis=1), N_EXPERTS - 1)
    n_used = (ends[-1:] // EXP_TM).astype(I32)

    xs = _dispatch(cnt8, lstart, gstart, lpos, hf, n_tiles * EXP_TM, TOK_TM)
    ys = _experts(tile_e, n_used, xs, w1[l], b1[l][:, None, :], w2[l], b2[l][:, None, :], EXP_TM)
    out = _combine(cnt8, lstart, gstart, x1, meta, gf, final_norm_w[None, :], ys, n, TOK_TM)
    return out.reshape(B, n, d)
```

```python
import functools
import math

import jax
import jax.numpy as jnp
from jax import lax
from jax.experimental import pallas as pl
from jax.experimental.pallas import tpu as pltpu

F32 = jnp.float32
BF16 = jnp.bfloat16
I32 = jnp.int32

D_MODEL = 1024
GRID_W = 64
EPS = 1e-6
ROPE_THETA = 10000.0

HEADS = 4
DIFF_HEAD_DIM = 64
HEAD_V = 128
DIFF_SCALE = DIFF_HEAD_DIM ** -0.5
MLA_NOPE = 128
MLA_ROPE = 64
MLA_QK = 256
MLA_Q_RANK = 256
MLA_KV_RANK = 128
MLA_SCALE = (MLA_NOPE + MLA_ROPE) ** -0.5
LOG2E = math.log2(math.e)
ROPE_PAIRS = 16
GROUP_WIDTH = HEADS * HEAD_V

N_EXPERTS = 32
TOP_K = 4
D_FF = 1024
SWIGLU_LIMIT = 7.0
SWIGLU_ALPHA = 1.702

LANES = 128
DMA_THREADS = 2
VMEM_LIMIT = 56 * 1024 * 1024

PRE_TM = 512
ATT_TQ_DIFF = 512
ATT_TQ_MLA = 1024
ATT_CHUNKS_PER_ITER = 1
TOK_TM = 512
EXP_TM = 512
SEG_ALIGN = 16
SEG_MAX = TOK_TM
SEG_BLOCKS = tuple(SEG_ALIGN << b for b in reversed(range((SEG_MAX // SEG_ALIGN).bit_length())))


def _local_rows(tm):
    rows = TOP_K * tm + N_EXPERTS * (SEG_ALIGN - 1)
    return -(-rows // LANES) * LANES


def _cparams(sem, **kw):
    return pltpu.CompilerParams(dimension_semantics=sem, vmem_limit_bytes=VMEM_LIMIT, **kw)


def _rms(x, w):
    return x * lax.rsqrt(jnp.mean(x * x, axis=-1, keepdims=True) + EPS) * w


def _dot(a, b):
    return jnp.dot(a, b, preferred_element_type=F32)


def _dot_nt(a, b):
    return lax.dot_general(a, b, (((1,), (1,)), ((), ())), preferred_element_type=F32)


def _split_bf16(x):
    hi = x.astype(BF16)
    lo = (x - hi.astype(F32)).astype(BF16)
    return hi, lo


def _ada_kernel(c_ref, w_ref, b_ref, o_ref):
    c = c_ref[...]
    s = c / (1.0 + jnp.exp(-c))
    s_hi, s_lo = _split_bf16(s)
    w_hi, w_lo = _split_bf16(w_ref[...])
    o_ref[...] = _dot(s_hi, w_hi) + _dot(s_lo, w_hi) + _dot(s_hi, w_lo) + b_ref[...]


def _ada(cc, w, b):
    rows, d = cc.shape
    n_out = w.shape[1]
    tn = 1536
    return pl.pallas_call(
        _ada_kernel,
        out_shape=jax.ShapeDtypeStruct((rows, n_out), F32),
        grid=(n_out // tn,),
        in_specs=[pl.BlockSpec((rows, d), lambda j: (0, 0)),
                  pl.BlockSpec((d, tn), lambda j: (0, j)),
                  pl.BlockSpec((1, tn), lambda j: (0, j))],
        out_specs=pl.BlockSpec((rows, tn), lambda j: (0, j)),
        compiler_params=_cparams(("arbitrary",)),
        name="adaln",
    )(cc, w, b)


C_DK, C_DV, C_CKV, C_KR = 0, 512, 1024, 1152
C_DQ, C_CQ, C_DKS, C_KRS, C_DQS = 1280, 1792, 2048, 2560, 2688
N_COLS_CTX = 1280
N_COLS = 3200


def _pre_kernel(with_q, *refs):
    if with_q:
        (x_ref, sh_ref, sc_ref, nw_ref, w_ref, qnw_ref, kvnw_ref, wuq_ref, wuqs_ref, wukv_ref,
         cd_ref, sd_ref, ck_ref, sk_ref,
         dk_ref, dvT_ref, mk_ref, mvT_ref, dqT_ref, mqT_ref) = refs
    else:
        (x_ref, sh_ref, sc_ref, nw_ref, w_ref, kvnw_ref, wukv_ref,
         dk_ref, dvT_ref, mk_ref, mvT_ref) = refs

    x = x_ref[0]
    h = _rms(x, nw_ref[...]) * (1.0 + sc_ref[0]) + sh_ref[0]
    hb = h.astype(BF16)

    def proj(c0, width):
        return _dot(hb, w_ref[:, c0:c0 + width])

    def rope(v, vs, c, s):
        return v * c + vs * s if with_q else v

    if with_q:
        cd, sd, ck, sk = cd_ref[...], sd_ref[...], ck_ref[...], sk_ref[...]
    else:
        cd = sd = ck = sk = None

    dk = proj(C_DK, GROUP_WIDTH)
    dks = proj(C_DKS, GROUP_WIDTH) if with_q else None
    dv = proj(C_DV, GROUP_WIDTH)
    for hd in range(HEADS):
        sl = slice(hd * HEAD_V, (hd + 1) * HEAD_V)
        dk_ref[0, hd] = rope(dk[:, sl], dks[:, sl] if with_q else None, cd, sd).astype(BF16)
        dvT_ref[0, hd, 0] = dv[:, sl].T.astype(BF16)

    ckv = _rms(proj(C_CKV, MLA_KV_RANK), kvnw_ref[...]).astype(BF16)
    kv = _dot(ckv, wukv_ref[...])
    kr = rope(proj(C_KR, LANES), proj(C_KRS, LANES) if with_q else None, ck, sk).astype(BF16)
    for hd in range(HEADS):
        c0 = hd * 2 * HEAD_V
        mk_ref[0, hd, :, 0:MLA_NOPE] = kv[:, c0:c0 + MLA_NOPE].astype(BF16)
        mk_ref[0, hd, :, MLA_NOPE:MLA_QK] = kr
        mvT_ref[0, hd, 0] = kv[:, c0 + MLA_NOPE:c0 + 2 * HEAD_V].T.astype(BF16)

    if with_q:
        dq = proj(C_DQ, GROUP_WIDTH)
        dqs = proj(C_DQS, GROUP_WIDTH)
        for hd in range(HEADS):
            sl = slice(hd * HEAD_V, (hd + 1) * HEAD_V)
            dqT_ref[0, hd] = ((dq[:, sl] * cd + dqs[:, sl] * sd) * (DIFF_SCALE * LOG2E)).T.astype(BF16)
        cq = _rms(proj(C_CQ, MLA_Q_RANK), qnw_ref[...]).astype(BF16)
        q = _dot(cq, wuq_ref[...])
        qs = _dot(cq, wuqs_ref[...])
        for hd in range(HEADS):
            c0 = hd * MLA_QK
            lo = q[:, c0:c0 + MLA_NOPE]
            hi = q[:, c0 + MLA_NOPE:c0 + MLA_QK] * ck + qs[:, hd * LANES:(hd + 1) * LANES] * sk
            mqT_ref[0, hd, 0:MLA_NOPE, :] = (lo * (MLA_SCALE * LOG2E)).T.astype(BF16)
            mqT_ref[0, hd, MLA_NOPE:MLA_QK, :] = (hi * (MLA_SCALE * LOG2E)).T.astype(BF16)


def _pre(x, shift, scale, nw, wts, tables, tm):
    B, n, d = x.shape
    with_q = tables is not None
    nt = n // tm
    per_b = shift.shape[0] > 1
    mod_map = (lambda b, j: (b, 0, 0)) if per_b else (lambda b, j: (0, 0, 0))
    full = lambda shape: pl.BlockSpec(shape, lambda b, j: (0,) * len(shape))
    ncols = N_COLS if with_q else N_COLS_CTX
    w_all = wts["w_all"] if with_q else wts["w_all"][:, :ncols]

    in_specs = [pl.BlockSpec((1, tm, d), lambda b, j: (b, j, 0)),
                pl.BlockSpec((1, 1, d), mod_map), pl.BlockSpec((1, 1, d), mod_map),
                full((1, d)), full((d, ncols))]
    args = [x, shift, scale, nw, w_all]
    if with_q:
        in_specs += [full((1, MLA_Q_RANK)), full((1, MLA_KV_RANK)),
                     full((MLA_Q_RANK, HEADS * MLA_QK)), full((MLA_Q_RANK, HEADS * LANES)),
                     full((MLA_KV_RANK, HEADS * 2 * HEAD_V))]
        args += [wts["qnw"], wts["kvnw"], wts["wuq"], wts["wuqs"], wts["wukv"]]
        in_specs += [pl.BlockSpec((tm, LANES), lambda b, j: (j, 0))] * 4
        args += list(tables)
    else:
        in_specs += [full((1, MLA_KV_RANK)), full((MLA_KV_RANK, HEADS * 2 * HEAD_V))]
        args += [wts["kvnw"], wts["wukv"]]

    row_map = lambda b, j: (b, 0, j, 0)
    chunk_map = lambda b, j: (b, 0, j, 0, 0)
    out_shape = [jax.ShapeDtypeStruct((B, HEADS, n, HEAD_V), BF16),
                 jax.ShapeDtypeStruct((B, HEADS, nt, HEAD_V, tm), BF16),
                 jax.ShapeDtypeStruct((B, HEADS, n, MLA_QK), BF16),
                 jax.ShapeDtypeStruct((B, HEADS, nt, HEAD_V, tm), BF16)]
    out_specs = [pl.BlockSpec((1, HEADS, tm, HEAD_V), row_map),
                 pl.BlockSpec((1, HEADS, 1, HEAD_V, tm), chunk_map),
                 pl.BlockSpec((1, HEADS, tm, MLA_QK), row_map),
                 pl.BlockSpec((1, HEADS, 1, HEAD_V, tm), chunk_map)]
    if with_q:
        col_map = lambda b, j: (b, 0, 0, j)
        out_shape += [jax.ShapeDtypeStruct((B, HEADS, HEAD_V, n), BF16),
                      jax.ShapeDtypeStruct((B, HEADS, MLA_QK, n), BF16)]
        out_specs += [pl.BlockSpec((1, HEADS, HEAD_V, tm), col_map),
                      pl.BlockSpec((1, HEADS, MLA_QK, tm), col_map)]
    return pl.pallas_call(
        functools.partial(_pre_kernel, with_q),
        out_shape=out_shape,
        grid=(B, nt),
        in_specs=in_specs,
        out_specs=out_specs,
        compiler_params=_cparams(("arbitrary", "arbitrary")),
        name="pre_attn_latent" if with_q else "pre_attn_ctx",
    )(*args)


def _attn_kernel(diff, lam_init, n_chunks, tk, cpi, *refs):
    if diff:
        (lam_ref, qT_ref, k_ref, vT_ref, kc_ref, vTc_ref, subw_ref, o_ref, s_ref, m_ref, l_ref, acc_ref) = refs
    else:
        (qT_ref, k_ref, vT_ref, kc_ref, vTc_ref, o_ref, s_ref, m_ref, l_ref, acc_ref) = refs
    qT = qT_ref[0, 0]
    tq = qT.shape[1]
    if diff:
        row = lax.broadcasted_iota(I32, qT.shape, 0)
        zero = jnp.zeros_like(qT)
        qT = jnp.concatenate([jnp.where(row < DIFF_HEAD_DIM, qT, zero),
                              jnp.where(row >= DIFF_HEAD_DIM, qT, zero)], axis=1)

    def softmax_pv(s, vTs):
        m_prev = m_ref[...]
        m_new = jnp.maximum(m_prev, jnp.max(s, axis=0, keepdims=True))
        p = jnp.exp2(s - m_new)
        psum = jnp.sum(p, axis=0, keepdims=True)
        pb = p.astype(BF16)
        pv, r0 = None, 0
        for vT in vTs:
            part = _dot(vT, pb[r0:r0 + vT.shape[1]])
            pv = part if pv is None else pv + part
            r0 += vT.shape[1]
        alpha = jnp.exp2(m_prev - m_new)
        l_ref[...] = alpha * l_ref[...] + psum
        acc_ref[...] = alpha * acc_ref[...] + pv
        m_ref[...] = m_new

    span = cpi * tk
    n_iters = n_chunks // cpi

    def scores(j):
        return _dot(k_ref[0, 0, pl.ds(pl.multiple_of(j * span, span), span), :], qT)

    def values(j):
        return [vT_ref[0, 0, j * cpi + c] for c in range(cpi)]

    m_ref[...] = jnp.full(m_ref.shape, -jnp.inf, F32)
    l_ref[...] = jnp.zeros(l_ref.shape, F32)
    acc_ref[...] = jnp.zeros(acc_ref.shape, F32)
    s_ref[0] = scores(0)

    def step(j, cur, nxt):
        s_next = scores(j + 1)
        softmax_pv(s_ref[cur], values(j))
        s_ref[nxt] = s_next

    def body(jj, carry):
        step(2 * jj, 0, 1)
        step(2 * jj + 1, 1, 0)
        return carry

    lax.fori_loop(0, (n_iters - 1) // 2, body, 0)
    last = 0
    if (n_iters - 1) % 2:
        step(n_iters - 2, 0, 1)
        last = 1
    s_last = jnp.concatenate([s_ref[last], _dot(kc_ref[0, 0], qT)], axis=0)
    softmax_pv(s_last, values(n_iters - 1) + [vTc_ref[0, 0, 0]])

    o = acc_ref[...] / l_ref[...]
    if diff:
        o = o[:, :tq] - lam_ref[0] * o[:, tq:]
        o = o * lax.rsqrt(jnp.mean(o * o, axis=0, keepdims=True) + EPS) * subw_ref[...] * (1.0 - lam_init)
    o_ref[0] = o.T.astype(BF16)


def _attention(diff, lam, lam_init, qT, k, vT, kc, vTc, subw, tq):
    B, H, dk, n = qT.shape
    n_chunks, tk = vT.shape[2], vT.shape[4]
    tc = vTc.shape[4]
    cpi = ATT_CHUNKS_PER_ITER if n_chunks % ATT_CHUNKS_PER_ITER == 0 else 1
    cols = 2 * tq if diff else tq
    bh4 = lambda b, h, i: (b, h, 0, 0)
    bh5 = lambda b, h, i: (b, h, 0, 0, 0)
    in_specs = [pl.BlockSpec((1, 1, dk, tq), lambda b, h, i: (b, h, 0, i)),
                pl.BlockSpec((1, 1, n, dk), bh4),
                pl.BlockSpec((1, 1, n_chunks, HEAD_V, tk), bh5),
                pl.BlockSpec((1, 1, tc, dk), bh4),
                pl.BlockSpec((1, 1, 1, HEAD_V, tc), bh5)]
    args = [qT, k, vT, kc, vTc]
    if diff:
        in_specs = [pl.BlockSpec(memory_space=pltpu.SMEM)] + in_specs + [
            pl.BlockSpec((HEAD_V, 1), lambda b, h, i: (0, 0))]
        args = [lam] + args + [subw]
    return pl.pallas_call(
        functools.partial(_attn_kernel, diff, lam_init, n_chunks, tk, cpi),
        out_shape=jax.ShapeDtypeStruct((B, n, H * HEAD_V), BF16),
        grid=(B, H, n // tq),
        in_specs=in_specs,
        out_specs=pl.BlockSpec((1, tq, HEAD_V), lambda b, h, i: (b, i, h)),
        scratch_shapes=[pltpu.VMEM((2, cpi * tk, cols), F32), pltpu.VMEM((1, cols), F32), pltpu.VMEM((1, cols), F32),
                        pltpu.VMEM((HEAD_V, cols), F32)],
        compiler_params=_cparams(("arbitrary", "arbitrary", "arbitrary")),
        name="attn_diff" if diff else "attn_mla",
    )(*args)


def _post_kernel(od_ref, om_ref, x_ref, ga_ref, wout_ref, fw_ref, sf_ref, scf_ref, rwT_ref, rb_ref,
                 x1_ref, hf_ref, lpos_ref, meta_ref, seg_ref, carry_ref):
    i = pl.program_id(0)

    @pl.when(i == 0)
    def _():
        carry_ref[...] = jnp.zeros(carry_ref.shape, F32)

    mix = jnp.concatenate([od_ref[...], om_ref[...]], axis=1)
    x1 = x_ref[...] + ga_ref[0] * _dot(mix, wout_ref[...])
    x1_ref[...] = x1
    hf = _rms(x1, fw_ref[...]) * (1.0 + scf_ref[0]) + sf_ref[0]
    hf_ref[...] = hf.astype(BF16)

    h_hi, h_lo = _split_bf16(hf)
    w_hi, w_lo = _split_bf16(rwT_ref[...])
    logits = _dot_nt(w_hi, h_hi) + _dot_nt(w_hi, h_lo) + _dot_nt(w_lo, h_hi) + rb_ref[...]
    n_e, tm = logits.shape
    eidx = lax.broadcasted_iota(I32, (n_e, tm), 0)

    vals, idxs = [], []
    cur = logits
    for _ in range(TOP_K):
        mx = jnp.max(cur, axis=0, keepdims=True)
        idx = jnp.min(jnp.where(cur == mx, eidx, n_e), axis=0, keepdims=True)
        vals.append(mx)
        idxs.append(idx)
        cur = jnp.where(eidx == idx, -jnp.inf, cur)
    exps = [jnp.exp(v - vals[0]) for v in vals]
    denom = exps[0] + exps[1] + exps[2] + exps[3]
    gates = jnp.concatenate([e / denom for e in exps], axis=0)

    onehot = jnp.zeros((n_e, tm), F32)
    for idx in idxs:
        onehot = onehot + jnp.where(eidx == idx, 1.0, 0.0)
    r = lax.broadcasted_iota(I32, (tm, tm), 0)
    c = lax.broadcasted_iota(I32, (tm, tm), 1)
    before = jnp.where(r < c, 1.0, 0.0).astype(BF16)
    prefix = _dot(onehot.astype(BF16), before)
    cnt8 = jnp.ceil(jnp.sum(onehot, axis=1, keepdims=True) * (1.0 / SEG_ALIGN)) * SEG_ALIGN
    er = lax.broadcasted_iota(I32, (n_e, n_e), 0)
    ec = lax.broadcasted_iota(I32, (n_e, n_e), 1)
    lower = jnp.where(ec < er, 1.0, 0.0).astype(BF16)
    lstart = _dot(lower, jnp.broadcast_to(cnt8, (n_e, LANES)).astype(BF16))[:, 0:1]
    where_in_tile = prefix + lstart
    lpos = jnp.concatenate(
        [jnp.sum(jnp.where(eidx == idx, where_in_tile, 0.0), axis=0, keepdims=True) for idx in idxs], axis=0)
    done = carry_ref[...]
    carry_ref[...] = done + cnt8

    lpos_ref[...] = lpos.astype(I32)
    meta = jnp.concatenate([gates, lpos, jnp.zeros((LANES - 2 * TOP_K, tm), F32)], axis=0)
    meta_ref[...] = meta.T
    lane = lax.broadcasted_iota(I32, (n_e, LANES), 1)
    seg_ref[0] = jnp.where(lane == 0, cnt8, jnp.where(lane == 1, lstart, jnp.where(lane == 2, done, 0.0)))


def _post(od, om, x, ga, wout, fw, sf, scf, rwT, rb, n_per_batch, tm):
    T, d = x.shape
    nt = T // tm
    per_b = n_per_batch // tm
    tok = lambda w: pl.BlockSpec((tm, w), lambda i: (i, 0))
    mod = pl.BlockSpec((1, 1, d), lambda i: (i // per_b, 0, 0))
    full = lambda shape: pl.BlockSpec(shape, lambda i: (0,) * len(shape))
    return pl.pallas_call(
        _post_kernel,
        out_shape=[jax.ShapeDtypeStruct((T, d), F32), jax.ShapeDtypeStruct((T, d), BF16),
                   jax.ShapeDtypeStruct((TOP_K, T), I32), jax.ShapeDtypeStruct((T, LANES), F32),
                   jax.ShapeDtypeStruct((nt, N_EXPERTS, LANES), F32)],
        grid=(nt,),
        in_specs=[tok(GROUP_WIDTH), tok(GROUP_WIDTH), tok(d), mod, full((d, d)), full((1, d)), mod, mod,
                  full((N_EXPERTS, d)), full((N_EXPERTS, 1))],
        out_specs=[tok(d), tok(d), pl.BlockSpec((TOP_K, tm), lambda i: (0, i)), tok(LANES),
                   pl.BlockSpec((1, N_EXPERTS, LANES), lambda i: (i, 0, 0))],
        scratch_shapes=[pltpu.VMEM((N_EXPERTS, 1), F32)],
        compiler_params=_cparams(("arbitrary",)),
        name="post_attn_router",
    )(od, om, x, ga, wout, fw, sf, scf, rwT, rb)


def _segment_copies(i, cnt_ref, ls_ref, gs_ref, local_ref, hbm_ref, sem, to_hbm, wait):
    def per_expert(e, carry):
        n, lo, go = cnt_ref[i, e], ls_ref[i, e], gs_ref[i, e]
        for b, size in enumerate(SEG_BLOCKS):
            larger = (n // (2 * size)) * (2 * size)

            @pl.when((n & size) != 0)
            def _():
                loc = local_ref.at[pl.ds(pl.multiple_of(lo + larger, SEG_ALIGN), size), :]
                glob = hbm_ref.at[pl.ds(pl.multiple_of(go + larger, SEG_ALIGN), size), :]
                cp = pltpu.make_async_copy(loc, glob, sem) if to_hbm else pltpu.make_async_copy(glob, loc, sem)
                if wait:
                    cp.wait()
                else:
                    cp.start(priority=b % DMA_THREADS)
        return carry

    lax.fori_loop(0, N_EXPERTS, per_expert, 0)


def _dispatch_kernel(cnt_ref, ls_ref, gs_ref, zcnt_ref, zls_ref, zgs_ref, lpos_ref, hf_ref, xs_ref,
                     sorted_ref, zero_ref, sems):
    i = pl.program_id(0)
    last = pl.num_programs(0) - 1
    slot = lax.rem(i, 2)

    @pl.when(i == 0)
    def _():
        zero_ref[...] = jnp.zeros(zero_ref.shape, BF16)
        _segment_copies(0, zcnt_ref, zls_ref, zgs_ref, zero_ref, xs_ref, sems.at[2], True, False)
        _segment_copies(0, zcnt_ref, zls_ref, zgs_ref, zero_ref, xs_ref, sems.at[2], True, True)

    lpos = lpos_ref[...]
    rows, tm = sorted_ref.shape[1], lpos.shape[1]
    r = lax.broadcasted_iota(I32, (rows, tm), 0)
    hit = r == lpos[0:1]
    for k in range(1, TOP_K):
        hit = jnp.logical_or(hit, r == lpos[k:k + 1])
    perm = jnp.where(hit, 1.0, 0.0).astype(BF16)
    sorted_ref[slot] = _dot(perm, hf_ref[...]).astype(BF16)

    @pl.when(i > 0)
    def _():
        _segment_copies(i - 1, cnt_ref, ls_ref, gs_ref, sorted_ref.at[1 - slot], xs_ref, sems.at[1 - slot], True, True)

    _segment_copies(i, cnt_ref, ls_ref, gs_ref, sorted_ref.at[slot], xs_ref, sems.at[slot], True, False)

    @pl.when(i == last)
    def _():
        _segment_copies(i, cnt_ref, ls_ref, gs_ref, sorted_ref.at[slot], xs_ref, sems.at[slot], True, True)


def _dispatch(cnt8, lstart, gstart, ztabs, lpos, hf, n_rows, tm):
    T, d = hf.shape
    return pl.pallas_call(
        _dispatch_kernel,
        out_shape=jax.ShapeDtypeStruct((n_rows, d), BF16),
        grid_spec=pltpu.PrefetchScalarGridSpec(
            num_scalar_prefetch=6,
            grid=(T // tm,),
            in_specs=[pl.BlockSpec((TOP_K, tm), lambda i, *_: (0, i)),
                      pl.BlockSpec((tm, d), lambda i, *_: (i, 0))],
            out_specs=pl.BlockSpec(memory_space=pl.ANY),
            scratch_shapes=[pltpu.VMEM((2, _local_rows(tm), d), BF16), pltpu.VMEM((EXP_TM, d), BF16),
                            pltpu.SemaphoreType.DMA((3,))]),
        compiler_params=_cparams(("arbitrary",)),
        name="moe_dispatch",
    )(cnt8, lstart, gstart, *ztabs, lpos, hf)


def _expert_kernel(te_ref, nu_ref, xs_ref, w1_ref, b1_ref, w2_ref, b2_ref, ys_ref, w1b_ref, w2b_ref):
    i = pl.program_id(0)
    active = i < nu_ref[0]
    new_expert = jnp.logical_or(i == 0, te_ref[i] != te_ref[jnp.maximum(i - 1, 0)])

    @pl.when(jnp.logical_and(active, new_expert))
    def _():
        w1b_ref[...] = w1_ref[0].astype(BF16)
        w2b_ref[...] = w2_ref[0].astype(BF16)

    @pl.when(active)
    def _():
        hid = _dot(xs_ref[...], w1b_ref[...]) + b1_ref[0]
        g = jnp.minimum(hid[:, :D_FF], SWIGLU_LIMIT)
        u = jnp.clip(hid[:, D_FF:], -SWIGLU_LIMIT, SWIGLU_LIMIT)
        act = g / (1.0 + jnp.exp(-SWIGLU_ALPHA * g)) * (u + 1.0)
        ys_ref[...] = (_dot(act.astype(BF16), w2b_ref[...]) + b2_ref[0]).astype(BF16)


def _experts(tile_e, n_used, xs, w1, b1, w2, b2, tm):
    n_rows, d = xs.shape
    nt = n_rows // tm
    row = lambda i, te, nu: (jnp.minimum(i, nu[0] - 1), 0)
    ex = lambda i, te, nu: (te[jnp.minimum(i, nu[0] - 1)], 0, 0)
    return pl.pallas_call(
        _expert_kernel,
        out_shape=jax.ShapeDtypeStruct((n_rows, d), BF16),
        grid_spec=pltpu.PrefetchScalarGridSpec(
            num_scalar_prefetch=2,
            grid=(nt,),
            in_specs=[pl.BlockSpec((tm, d), row),
                      pl.BlockSpec((1, d, 2 * D_FF), ex), pl.BlockSpec((1, 1, 2 * D_FF), ex),
                      pl.BlockSpec((1, D_FF, d), ex), pl.BlockSpec((1, 1, d), ex)],
            out_specs=pl.BlockSpec((tm, d), row),
            scratch_shapes=[pltpu.VMEM((d, 2 * D_FF), BF16), pltpu.VMEM((D_FF, d), BF16)]),
        compiler_params=_cparams(("arbitrary",)),
        name="moe_experts",
    )(tile_e, n_used, xs, w1, b1, w2, b2)


def _combine_kernel(cnt_ref, ls_ref, gs_ref, x1_ref, meta_ref, gf_ref, fnw_ref, ys_ref, o_ref, yl_ref, sems):
    i = pl.program_id(0)
    last = pl.num_programs(0) - 1
    slot = lax.rem(i, 2)

    def gather(tile, buf, wait):
        _segment_copies(tile, cnt_ref, ls_ref, gs_ref, yl_ref.at[buf], ys_ref, sems.at[buf], False, wait)

    @pl.when(i == 0)
    def _():
        yl_ref[...] = jnp.zeros(yl_ref.shape, BF16)
        gather(0, 0, False)

    @pl.when(i < last)
    def _():
        gather(i + 1, 1 - slot, False)

    gather(i, slot, True)

    meta = meta_ref[...]
    tm, rows = meta.shape[0], yl_ref.shape[1]
    col = lax.broadcasted_iota(I32, (tm, rows), 1).astype(F32)
    weights = jnp.zeros((tm, rows), F32)
    for k in range(TOP_K):
        weights = weights + jnp.where(col == meta[:, TOP_K + k:TOP_K + k + 1], meta[:, k:k + 1], 0.0)
    weights = weights.astype(BF16)
    x2 = x1_ref[...] + gf_ref[0] * _dot(weights, yl_ref[slot])
    o_ref[...] = _rms(x2, fnw_ref[...])


def _combine(cnt8, lstart, gstart, x1, meta, gf, fnw, ys, n_per_batch, tm):
    T, d = x1.shape
    per_b = n_per_batch // tm
    return pl.pallas_call(
        _combine_kernel,
        out_shape=jax.ShapeDtypeStruct((T, d), F32),
        grid_spec=pltpu.PrefetchScalarGridSpec(
            num_scalar_prefetch=3,
            grid=(T // tm,),
            in_specs=[pl.BlockSpec((tm, d), lambda i, *_: (i, 0)),
                      pl.BlockSpec((tm, LANES), lambda i, *_: (i, 0)),
                      pl.BlockSpec((1, 1, d), lambda i, *_: (i // per_b, 0, 0)),
                      pl.BlockSpec((1, d), lambda i, *_: (0, 0)),
                      pl.BlockSpec(memory_space=pl.ANY)],
            out_specs=pl.BlockSpec((tm, d), lambda i, *_: (i, 0)),
            scratch_shapes=[pltpu.VMEM((2, _local_rows(tm), d), BF16), pltpu.SemaphoreType.DMA((2,))]),
        compiler_params=_cparams(("arbitrary",)),
        name="moe_combine",
    )(cnt8, lstart, gstart, x1, meta, gf, fnw, ys)


def _rope_partner_perm(width):
    i = jnp.arange(width)
    blk = i // 64
    r = i % 64
    a, j, p = r // 32, (r // 16) % 2, r % 16
    return blk * 64 + a * 32 + (1 - j) * 16 + p


def _prep_weights(w_in, q_norm_w, kv_norm_w, w_uq, w_ukv):
    dq, dk, dv, cq, ckv, kr = jnp.split(w_in, [512, 1024, 1536, 1792, 1920], axis=1)
    d = w_in.shape[0]
    zeros64 = jnp.zeros((d, 64), F32)
    krp = jnp.concatenate([kr, zeros64], axis=1)
    perm512 = _rope_partner_perm(512)
    krs = jnp.concatenate([kr[:, _rope_partner_perm(64)], zeros64], axis=1)
    w_all = jnp.concatenate([dk, dv, ckv, krp, dq, cq, dk[:, perm512], krs, dq[:, perm512]], axis=1)
    wq = w_uq.reshape(MLA_Q_RANK, HEADS, MLA_NOPE + MLA_ROPE)
    z = jnp.zeros((MLA_Q_RANK, HEADS, 64), F32)
    wuq = jnp.concatenate([wq, z], axis=2).reshape(MLA_Q_RANK, HEADS * MLA_QK)
    wq_rope_sw = wq[:, :, MLA_NOPE:][:, :, _rope_partner_perm(64)]
    wuqs = jnp.concatenate([wq_rope_sw, z], axis=2).reshape(MLA_Q_RANK, HEADS * LANES)
    return {"w_all": w_all.astype(BF16), "qnw": q_norm_w[None, :], "kvnw": kv_norm_w[None, :],
            "wuq": wuq.astype(BF16), "wuqs": wuqs.astype(BF16), "wukv": w_ukv.astype(BF16)}


def _rope_tables(n):
    rows = n // GRID_W
    row = jnp.repeat(jnp.arange(rows, dtype=F32), GRID_W)
    col = jnp.tile(jnp.arange(GRID_W, dtype=F32), rows)
    inv = ROPE_THETA ** (-jnp.arange(ROPE_PAIRS, dtype=F32) / ROPE_PAIRS)
    ang = jnp.stack([row[:, None] * inv, col[:, None] * inv], axis=1)
    cos, sin = jnp.cos(ang), jnp.sin(ang)
    c64 = jnp.stack([cos, cos], axis=2).reshape(n, 64)
    s64 = jnp.stack([-sin, sin], axis=2).reshape(n, 64)
    one, zero = jnp.ones((n, 64), F32), jnp.zeros((n, 64), F32)
    return (jnp.concatenate([c64, c64], axis=1), jnp.concatenate([s64, s64], axis=1),
            jnp.concatenate([c64, one], axis=1), jnp.concatenate([s64, zero], axis=1))


def kernel(x, c, ctx, c_ctx, w_ada, b_ada, attn_norm_w, w_in, q_norm_w, kv_norm_w, w_uq, w_ukv,
           lambda_q1, lambda_k1, lambda_q2, lambda_k2, subln_w, w_out, ffn_norm_w,
           router_w, router_b, w1, b1, w2, b2, final_norm_w):
    B, n, d = x.shape
    T = B * n
    depth = w_ada.shape[0]
    assert depth == 1 and d == D_MODEL and n % PRE_TM == 0 and B + 1 <= 8
    l = 0
    lam_init = 0.8 - 0.6 * math.exp(-0.3 * l)
    lam = (jnp.exp(jnp.sum(lambda_q1[l] * lambda_k1[l])) - jnp.exp(jnp.sum(lambda_q2[l] * lambda_k2[l]))
           + lam_init).reshape(1).astype(F32)

    cc = jnp.concatenate([c, c_ctx[None, :], jnp.zeros((8 - B - 1, d), F32)], axis=0)
    mod = _ada(cc, w_ada[l], b_ada[l][None, :])
    sa, sca, ga, sf, scf, gf = [mod[:B, i * d:(i + 1) * d][:, None, :] for i in range(6)]
    csa, csca = [mod[B:B + 1, i * d:(i + 1) * d][:, None, :] for i in range(2)]

    wts = _prep_weights(w_in[l], q_norm_w[l], kv_norm_w[l], w_uq[l], w_ukv[l])
    anw = attn_norm_w[l][None, :]
    dk, dvT, mk, mvT, dqT, mqT = _pre(x, sa, sca, anw, wts, _rope_tables(n), PRE_TM)
    dkc, dvTc, mkc, mvTc = _pre(ctx, csa, csca, anw, wts, None, ctx.shape[1])

    o_diff = _attention(True, lam, lam_init, dqT, dk, dvT, dkc, dvTc, subln_w[l][:, None], ATT_TQ_DIFF)
    o_mla = _attention(False, None, lam_init, mqT, mk, mvT, mkc, mvTc, None, ATT_TQ_MLA)

    x1, hf, lpos, meta, seg = _post(
        o_diff.reshape(T, GROUP_WIDTH), o_mla.reshape(T, GROUP_WIDTH), x.reshape(T, d), ga,
        w_out[l].astype(BF16), ffn_norm_w[l][None, :], sf, scf,
        router_w[l].T, router_b[l][:, None], n, TOK_TM)

    cnt8, lstart, done = [seg[:, :, j].astype(I32) for j in range(3)]
    totals = done[-1] + cnt8[-1]
    padded = ((totals + EXP_TM - 1) // EXP_TM) * EXP_TM
    ends = jnp.cumsum(padded)
    gstart = (ends - padded)[None, :] + done
    ztabs = ((padded - totals)[None, :], jnp.zeros((1, N_EXPERTS), I32), (ends - padded + totals)[None, :])
    n_tok_tiles = T // TOK_TM
    n_tiles = -(-(T * TOP_K + n_tok_tiles * N_EXPERTS * (SEG_ALIGN - 1)) // EXP_TM) + N_EXPERTS
    tile_start = jnp.arange(n_tiles, dtype=I32) * EXP_TM
    tile_e = jnp.minimum(jnp.sum((ends[None, :] <= tile_start[:, None]).astype(I32), axis=1), N_EXPERTS - 1)
    n_used = (ends[-1:] // EXP_TM).astype(I32)

    xs = _dispatch(cnt8, lstart, gstart, ztabs, lpos, hf, n_tiles * EXP_TM, TOK_TM)
    ys = _experts(tile_e, n_used, xs, w1[l], b1[l][:, None, :], w2[l], b2[l][:, None, :], EXP_TM)
    out = _combine(cnt8, lstart, gstart, x1, meta, gf, final_norm_w[None, :], ys, n, TOK_TM)
    return out.reshape(B, n, d)
```

```python
import functools
import math

import jax
import jax.numpy as jnp
from jax import lax
from jax.experimental import pallas as pl
from jax.experimental.pallas import tpu as pltpu

F32 = jnp.float32
BF16 = jnp.bfloat16
I32 = jnp.int32

D_MODEL = 1024
GRID_W = 64
EPS = 1e-6
ROPE_THETA = 10000.0

HEADS = 4
DIFF_HEAD_DIM = 64
HEAD_V = 128
DIFF_SCALE = DIFF_HEAD_DIM ** -0.5
MLA_NOPE = 128
MLA_ROPE = 64
MLA_QK = 256
MLA_Q_RANK = 256
MLA_KV_RANK = 128
MLA_SCALE = (MLA_NOPE + MLA_ROPE) ** -0.5
LOG2E = math.log2(math.e)
ROPE_PAIRS = 16
GROUP_WIDTH = HEADS * HEAD_V

N_EXPERTS = 32
TOP_K = 4
D_FF = 1024
SWIGLU_LIMIT = 7.0
SWIGLU_ALPHA = 1.702

LANES = 128
DMA_THREADS = 2
VMEM_LIMIT = 56 * 1024 * 1024

PRE_TM = 512
ATT_TQ_DIFF = 512
ATT_TQ_MLA = 1024
ATT_CHUNKS_PER_ITER = 1
TOK_TM = 512
EXP_TM = 512
SEG_ALIGN = 16
SEG_MAX = TOK_TM
SEG_BLOCKS = tuple(SEG_ALIGN << b for b in reversed(range((SEG_MAX // SEG_ALIGN).bit_length())))


def _local_rows(tm):
    rows = TOP_K * tm + N_EXPERTS * (SEG_ALIGN - 1)
    return -(-rows // LANES) * LANES


def _cparams(sem, **kw):
    return pltpu.CompilerParams(dimension_semantics=sem, vmem_limit_bytes=VMEM_LIMIT, **kw)


def _rms(x, w):
    return x * lax.rsqrt(jnp.mean(x * x, axis=-1, keepdims=True) + EPS) * w


def _dot(a, b):
    return jnp.dot(a, b, preferred_element_type=F32)


def _dot_nt(a, b):
    return lax.dot_general(a, b, (((1,), (1,)), ((), ())), preferred_element_type=F32)


def _split_bf16(x):
    hi = x.astype(BF16)
    lo = (x - hi.astype(F32)).astype(BF16)
    return hi, lo


def _ada_kernel(c_ref, w_ref, b_ref, o_ref):
    c = c_ref[...]
    s = c / (1.0 + jnp.exp(-c))
    s_hi, s_lo = _split_bf16(s)
    w_hi, w_lo = _split_bf16(w_ref[...])
    o_ref[...] = _dot(s_hi, w_hi) + _dot(s_lo, w_hi) + _dot(s_hi, w_lo) + b_ref[...]


def _ada(cc, w, b):
    rows, d = cc.shape
    n_out = w.shape[1]
    tn = 1536
    return pl.pallas_call(
        _ada_kernel,
        out_shape=jax.ShapeDtypeStruct((rows, n_out), F32),
        grid=(n_out // tn,),
        in_specs=[pl.BlockSpec((rows, d), lambda j: (0, 0)),
                  pl.BlockSpec((d, tn), lambda j: (0, j)),
                  pl.BlockSpec((1, tn), lambda j: (0, j))],
        out_specs=pl.BlockSpec((rows, tn), lambda j: (0, j)),
        compiler_params=_cparams(("arbitrary",)),
        name="adaln",
    )(cc, w, b)


C_DK, C_DV, C_CKV, C_KR = 0, 512, 1024, 1152
C_DQ, C_CQ, C_DKS, C_KRS, C_DQS = 1280, 1792, 2048, 2560, 2688
N_COLS_CTX = 1280
N_COLS = 3200


def _pre_kernel(with_q, *refs):
    if with_q:
        (x_ref, sh_ref, sc_ref, nw_ref, w_ref, qnw_ref, kvnw_ref, wuq_ref, wuqs_ref, wukv_ref,
         cd_ref, sd_ref, ck_ref, sk_ref,
         dk_ref, dvT_ref, mk_ref, mvT_ref, dqT_ref, mqT_ref) = refs
    else:
        (x_ref, sh_ref, sc_ref, nw_ref, w_ref, kvnw_ref, wukv_ref,
         dk_ref, dvT_ref, mk_ref, mvT_ref) = refs

    x = x_ref[0]
    h = _rms(x, nw_ref[...]) * (1.0 + sc_ref[0]) + sh_ref[0]
    hb = h.astype(BF16)

    def proj(c0, width):
        return _dot(hb, w_ref[:, c0:c0 + width])

    def rope(v, vs, c, s):
        return v * c + vs * s if with_q else v

    if with_q:
        cd, sd, ck, sk = cd_ref[...], sd_ref[...], ck_ref[...], sk_ref[...]
    else:
        cd = sd = ck = sk = None

    dk = proj(C_DK, GROUP_WIDTH)
    dks = proj(C_DKS, GROUP_WIDTH) if with_q else None
    dv = proj(C_DV, GROUP_WIDTH)
    for hd in range(HEADS):
        sl = slice(hd * HEAD_V, (hd + 1) * HEAD_V)
        dk_ref[0, hd] = rope(dk[:, sl], dks[:, sl] if with_q else None, cd, sd).astype(BF16)
        dvT_ref[0, hd, 0] = dv[:, sl].T.astype(BF16)

    ckv = _rms(proj(C_CKV, MLA_KV_RANK), kvnw_ref[...]).astype(BF16)
    kv = _dot(ckv, wukv_ref[...])
    kr = rope(proj(C_KR, LANES), proj(C_KRS, LANES) if with_q else None, ck, sk).astype(BF16)
    for hd in range(HEADS):
        c0 = hd * 2 * HEAD_V
        mk_ref[0, hd, :, 0:MLA_NOPE] = kv[:, c0:c0 + MLA_NOPE].astype(BF16)
        mk_ref[0, hd, :, MLA_NOPE:MLA_QK] = kr
        mvT_ref[0, hd, 0] = kv[:, c0 + MLA_NOPE:c0 + 2 * HEAD_V].T.astype(BF16)

    if with_q:
        dq = proj(C_DQ, GROUP_WIDTH)
        dqs = proj(C_DQS, GROUP_WIDTH)
        for hd in range(HEADS):
            sl = slice(hd * HEAD_V, (hd + 1) * HEAD_V)
            dqT_ref[0, hd] = ((dq[:, sl] * cd + dqs[:, sl] * sd) * (DIFF_SCALE * LOG2E)).T.astype(BF16)
        cq = _rms(proj(C_CQ, MLA_Q_RANK), qnw_ref[...]).astype(BF16)
        q = _dot(cq, wuq_ref[...])
        qs = _dot(cq, wuqs_ref[...])
        for hd in range(HEADS):
            c0 = hd * MLA_QK
            lo = q[:, c0:c0 + MLA_NOPE]
            hi = q[:, c0 + MLA_NOPE:c0 + MLA_QK] * ck + qs[:, hd * LANES:(hd + 1) * LANES] * sk
            mqT_ref[0, hd, 0:MLA_NOPE, :] = (lo * (MLA_SCALE * LOG2E)).T.astype(BF16)
            mqT_ref[0, hd, MLA_NOPE:MLA_QK, :] = (hi * (MLA_SCALE * LOG2E)).T.astype(BF16)


def _pre(x, shift, scale, nw, wts, tables, tm):
    B, n, d = x.shape
    with_q = tables is not None
    nt = n // tm
    per_b = shift.shape[0] > 1
    mod_map = (lambda b, j: (b, 0, 0)) if per_b else (lambda b, j: (0, 0, 0))
    full = lambda shape: pl.BlockSpec(shape, lambda b, j: (0,) * len(shape))
    ncols = N_COLS if with_q else N_COLS_CTX
    w_all = wts["w_all"] if with_q else wts["w_all"][:, :ncols]

    in_specs = [pl.BlockSpec((1, tm, d), lambda b, j: (b, j, 0)),
                pl.BlockSpec((1, 1, d), mod_map), pl.BlockSpec((1, 1, d), mod_map),
                full((1, d)), full((d, ncols))]
    args = [x, shift, scale, nw, w_all]
    if with_q:
        in_specs += [full((1, MLA_Q_RANK)), full((1, MLA_KV_RANK)),
                     full((MLA_Q_RANK, HEADS * MLA_QK)), full((MLA_Q_RANK, HEADS * LANES)),
                     full((MLA_KV_RANK, HEADS * 2 * HEAD_V))]
        args += [wts["qnw"], wts["kvnw"], wts["wuq"], wts["wuqs"], wts["wukv"]]
        in_specs += [pl.BlockSpec((tm, LANES), lambda b, j: (j, 0))] * 4
        args += list(tables)
    else:
        in_specs += [full((1, MLA_KV_RANK)), full((MLA_KV_RANK, HEADS * 2 * HEAD_V))]
        args += [wts["kvnw"], wts["wukv"]]

    row_map = lambda b, j: (b, 0, j, 0)
    chunk_map = lambda b, j: (b, 0, j, 0, 0)
    out_shape = [jax.ShapeDtypeStruct((B, HEADS, n, HEAD_V), BF16),
                 jax.ShapeDtypeStruct((B, HEADS, nt, HEAD_V, tm), BF16),
                 jax.ShapeDtypeStruct((B, HEADS, n, MLA_QK), BF16),
                 jax.ShapeDtypeStruct((B, HEADS, nt, HEAD_V, tm), BF16)]
    out_specs = [pl.BlockSpec((1, HEADS, tm, HEAD_V), row_map),
                 pl.BlockSpec((1, HEADS, 1, HEAD_V, tm), chunk_map),
                 pl.BlockSpec((1, HEADS, tm, MLA_QK), row_map),
                 pl.BlockSpec((1, HEADS, 1, HEAD_V, tm), chunk_map)]
    if with_q:
        col_map = lambda b, j: (b, 0, 0, j)
        out_shape += [jax.ShapeDtypeStruct((B, HEADS, HEAD_V, n), BF16),
                      jax.ShapeDtypeStruct((B, HEADS, MLA_QK, n), BF16)]
        out_specs += [pl.BlockSpec((1, HEADS, HEAD_V, tm), col_map),
                      pl.BlockSpec((1, HEADS, MLA_QK, tm), col_map)]
    return pl.pallas_call(
        functools.partial(_pre_kernel, with_q),
        out_shape=out_shape,
        grid=(B, nt),
        in_specs=in_specs,
        out_specs=out_specs,
        compiler_params=_cparams(("arbitrary", "arbitrary")),
        name="pre_attn_latent" if with_q else "pre_attn_ctx",
    )(*args)


def _attn_kernel(diff, lam_init, n_chunks, tk, cpi, *refs):
    if diff:
        (lam_ref, qT_ref, k_ref, vT_ref, kc_ref, vTc_ref, subw_ref, o_ref, s_ref, m_ref, l_ref, acc_ref) = refs
    else:
        (qT_ref, k_ref, vT_ref, kc_ref, vTc_ref, o_ref, s_ref, m_ref, l_ref, acc_ref) = refs
    qT = qT_ref[0, 0]
    tq = qT.shape[1]
    if diff:
        row = lax.broadcasted_iota(I32, qT.shape, 0)
        zero = jnp.zeros_like(qT)
        qT = jnp.concatenate([jnp.where(row < DIFF_HEAD_DIM, qT, zero),
                              jnp.where(row >= DIFF_HEAD_DIM, qT, zero)], axis=1)

    def softmax_pv(s, vTs):
        m_prev = m_ref[...]
        m_new = jnp.maximum(m_prev, jnp.max(s, axis=0, keepdims=True))
        p = jnp.exp2(s - m_new)
        psum = jnp.sum(p, axis=0, keepdims=True)
        pb = p.astype(BF16)
        pv, r0 = None, 0
        for vT in vTs:
            part = _dot(vT, pb[r0:r0 + vT.shape[1]])
            pv = part if pv is None else pv + part
            r0 += vT.shape[1]
        alpha = jnp.exp2(m_prev - m_new)
        l_ref[...] = alpha * l_ref[...] + psum
        acc_ref[...] = alpha * acc_ref[...] + pv
        m_ref[...] = m_new

    span = cpi * tk
    n_iters = n_chunks // cpi

    def scores(j):
        return _dot(k_ref[0, 0, pl.ds(pl.multiple_of(j * span, span), span), :], qT)

    def values(j):
        return [vT_ref[0, 0, j * cpi + c] for c in range(cpi)]

    m_ref[...] = jnp.full(m_ref.shape, -jnp.inf, F32)
    l_ref[...] = jnp.zeros(l_ref.shape, F32)
    acc_ref[...] = jnp.zeros(acc_ref.shape, F32)
    s_ctx = _dot(kc_ref[0, 0], qT)
    s_first = scores(0)
    softmax_pv(s_ctx, [vTc_ref[0, 0, 0]])
    s_ref[0] = s_first

    def step(j, cur, nxt):
        s_next = scores(j + 1)
        softmax_pv(s_ref[cur], values(j))
        s_ref[nxt] = s_next

    def body(jj, carry):
        step(2 * jj, 0, 1)
        step(2 * jj + 1, 1, 0)
        return carry

    lax.fori_loop(0, (n_iters - 1) // 2, body, 0)
    last = 0
    if (n_iters - 1) % 2:
        step(n_iters - 2, 0, 1)
        last = 1
    softmax_pv(s_ref[last], values(n_iters - 1))

    o = acc_ref[...] / l_ref[...]
    if diff:
        o = o[:, :tq] - lam_ref[0] * o[:, tq:]
        o = o * lax.rsqrt(jnp.mean(o * o, axis=0, keepdims=True) + EPS) * subw_ref[...] * (1.0 - lam_init)
    o_ref[0] = o.T.astype(BF16)


def _attention(diff, lam, lam_init, qT, k, vT, kc, vTc, subw, tq):
    B, H, dk, n = qT.shape
    n_chunks, tk = vT.shape[2], vT.shape[4]
    tc = vTc.shape[4]
    cpi = ATT_CHUNKS_PER_ITER if n_chunks % ATT_CHUNKS_PER_ITER == 0 else 1
    cols = 2 * tq if diff else tq
    bh4 = lambda b, h, i: (b, h, 0, 0)
    bh5 = lambda b, h, i: (b, h, 0, 0, 0)
    in_specs = [pl.BlockSpec((1, 1, dk, tq), lambda b, h, i: (b, h, 0, i)),
                pl.BlockSpec((1, 1, n, dk), bh4),
                pl.BlockSpec((1, 1, n_chunks, HEAD_V, tk), bh5),
                pl.BlockSpec((1, 1, tc, dk), bh4),
                pl.BlockSpec((1, 1, 1, HEAD_V, tc), bh5)]
    args = [qT, k, vT, kc, vTc]
    if diff:
        in_specs = [pl.BlockSpec(memory_space=pltpu.SMEM)] + in_specs + [
            pl.BlockSpec((HEAD_V, 1), lambda b, h, i: (0, 0))]
        args = [lam] + args + [subw]
    return pl.pallas_call(
        functools.partial(_attn_kernel, diff, lam_init, n_chunks, tk, cpi),
        out_shape=jax.ShapeDtypeStruct((B, n, H * HEAD_V), BF16),
        grid=(B, H, n // tq),
        in_specs=in_specs,
        out_specs=pl.BlockSpec((1, tq, HEAD_V), lambda b, h, i: (b, i, h)),
        scratch_shapes=[pltpu.VMEM((2, cpi * tk, cols), F32), pltpu.VMEM((1, cols), F32), pltpu.VMEM((1, cols), F32),
                        pltpu.VMEM((HEAD_V, cols), F32)],
        compiler_params=_cparams(("arbitrary", "arbitrary", "arbitrary")),
        name="attn_diff" if diff else "attn_mla",
    )(*args)


def _post_kernel(od_ref, om_ref, x_ref, ga_ref, wout_ref, fw_ref, sf_ref, scf_ref, rwT_ref, rb_ref,
                 x1_ref, hf_ref, lpos_ref, meta_ref, seg_ref, carry_ref):
    i = pl.program_id(0)

    @pl.when(i == 0)
    def _():
        carry_ref[...] = jnp.zeros(carry_ref.shape, F32)

    mix = jnp.concatenate([od_ref[...], om_ref[...]], axis=1)
    x1 = x_ref[...] + ga_ref[0] * _dot(mix, wout_ref[...])
    x1_ref[...] = x1
    hf = _rms(x1, fw_ref[...]) * (1.0 + scf_ref[0]) + sf_ref[0]
    hf_ref[...] = hf.astype(BF16)

    h_hi, h_lo = _split_bf16(hf)
    w_hi, w_lo = _split_bf16(rwT_ref[...])
    logits = _dot_nt(w_hi, h_hi) + _dot_nt(w_hi, h_lo) + _dot_nt(w_lo, h_hi) + rb_ref[...]
    n_e, tm = logits.shape
    eidx = lax.broadcasted_iota(I32, (n_e, tm), 0)

    vals, idxs = [], []
    cur = logits
    for _ in range(TOP_K):
        mx = jnp.max(cur, axis=0, keepdims=True)
        idx = jnp.min(jnp.where(cur == mx, eidx, n_e), axis=0, keepdims=True)
        vals.append(mx)
        idxs.append(idx)
        cur = jnp.where(eidx == idx, -jnp.inf, cur)
    exps = [jnp.exp(v - vals[0]) for v in vals]
    denom = exps[0] + exps[1] + exps[2] + exps[3]
    gates = jnp.concatenate([e / denom for e in exps], axis=0)

    onehot = jnp.zeros((n_e, tm), F32)
    for idx in idxs:
        onehot = onehot + jnp.where(eidx == idx, 1.0, 0.0)
    r = lax.broadcasted_iota(I32, (tm, tm), 0)
    c = lax.broadcasted_iota(I32, (tm, tm), 1)
    before = jnp.where(r < c, 1.0, 0.0).astype(BF16)
    prefix = _dot(onehot.astype(BF16), before)
    cnt8 = jnp.ceil(jnp.sum(onehot, axis=1, keepdims=True) * (1.0 / SEG_ALIGN)) * SEG_ALIGN
    er = lax.broadcasted_iota(I32, (n_e, n_e), 0)
    ec = lax.broadcasted_iota(I32, (n_e, n_e), 1)
    lower = jnp.where(ec < er, 1.0, 0.0).astype(BF16)
    lstart = _dot(lower, jnp.broadcast_to(cnt8, (n_e, LANES)).astype(BF16))[:, 0:1]
    where_in_tile = prefix + lstart
    lpos = jnp.concatenate(
        [jnp.sum(jnp.where(eidx == idx, where_in_tile, 0.0), axis=0, keepdims=True) for idx in idxs], axis=0)
    done = carry_ref[...]
    carry_ref[...] = done + cnt8

    lpos_ref[...] = lpos.astype(I32)
    meta = jnp.concatenate([gates, lpos, jnp.zeros((LANES - 2 * TOP_K, tm), F32)], axis=0)
    meta_ref[...] = meta.T
    lane = lax.broadcasted_iota(I32, (n_e, LANES), 1)
    seg_ref[0] = jnp.where(lane == 0, cnt8, jnp.where(lane == 1, lstart, jnp.where(lane == 2, done, 0.0)))


def _post(od, om, x, ga, wout, fw, sf, scf, rwT, rb, n_per_batch, tm):
    T, d = x.shape
    nt = T // tm
    per_b = n_per_batch // tm
    tok = lambda w: pl.BlockSpec((tm, w), lambda i: (i, 0))
    mod = pl.BlockSpec((1, 1, d), lambda i: (i // per_b, 0, 0))
    full = lambda shape: pl.BlockSpec(shape, lambda i: (0,) * len(shape))
    return pl.pallas_call(
        _post_kernel,
        out_shape=[jax.ShapeDtypeStruct((T, d), F32), jax.ShapeDtypeStruct((T, d), BF16),
                   jax.ShapeDtypeStruct((TOP_K, T), I32), jax.ShapeDtypeStruct((T, LANES), F32),
                   jax.ShapeDtypeStruct((nt, N_EXPERTS, LANES), F32)],
        grid=(nt,),
        in_specs=[tok(GROUP_WIDTH), tok(GROUP_WIDTH), tok(d), mod, full((d, d)), full((1, d)), mod, mod,
                  full((N_EXPERTS, d)), full((N_EXPERTS, 1))],
        out_specs=[tok(d), tok(d), pl.BlockSpec((TOP_K, tm), lambda i: (0, i)), tok(LANES),
                   pl.BlockSpec((1, N_EXPERTS, LANES), lambda i: (i, 0, 0))],
        scratch_shapes=[pltpu.VMEM((N_EXPERTS, 1), F32)],
        compiler_params=_cparams(("arbitrary",)),
        name="post_attn_router",
    )(od, om, x, ga, wout, fw, sf, scf, rwT, rb)


def _segment_copies(i, cnt_ref, ls_ref, gs_ref, local_ref, hbm_ref, sem, to_hbm, wait):
    def per_expert(e, carry):
        n, lo, go = cnt_ref[i, e], ls_ref[i, e], gs_ref[i, e]
        for b, size in enumerate(SEG_BLOCKS):
            larger = (n // (2 * size)) * (2 * size)

            @pl.when((n & size) != 0)
            def _():
                loc = local_ref.at[pl.ds(pl.multiple_of(lo + larger, SEG_ALIGN), size), :]
                glob = hbm_ref.at[pl.ds(pl.multiple_of(go + larger, SEG_ALIGN), size), :]
                cp = pltpu.make_async_copy(loc, glob, sem) if to_hbm else pltpu.make_async_copy(glob, loc, sem)
                if wait:
                    cp.wait()
                else:
                    cp.start(priority=b % DMA_THREADS)
        return carry

    lax.fori_loop(0, N_EXPERTS, per_expert, 0)


def _dispatch_kernel(cnt_ref, ls_ref, gs_ref, zcnt_ref, zls_ref, zgs_ref, lpos_ref, hf_ref, xs_ref,
                     sorted_ref, zero_ref, sems):
    i = pl.program_id(0)
    last = pl.num_programs(0) - 1
    slot = lax.rem(i, 2)

    @pl.when(i == 0)
    def _():
        zero_ref[...] = jnp.zeros(zero_ref.shape, BF16)
        _segment_copies(0, zcnt_ref, zls_ref, zgs_ref, zero_ref, xs_ref, sems.at[2], True, False)
        _segment_copies(0, zcnt_ref, zls_ref, zgs_ref, zero_ref, xs_ref, sems.at[2], True, True)

    lpos = lpos_ref[...]
    rows, tm = sorted_ref.shape[1], lpos.shape[1]
    r = lax.broadcasted_iota(I32, (rows, tm), 0)
    hit = r == lpos[0:1]
    for k in range(1, TOP_K):
        hit = jnp.logical_or(hit, r == lpos[k:k + 1])
    perm = jnp.where(hit, 1.0, 0.0).astype(BF16)
    sorted_ref[slot] = _dot(perm, hf_ref[...]).astype(BF16)

    @pl.when(i > 0)
    def _():
        _segment_copies(i - 1, cnt_ref, ls_ref, gs_ref, sorted_ref.at[1 - slot], xs_ref, sems.at[1 - slot], True, True)

    _segment_copies(i, cnt_ref, ls_ref, gs_ref, sorted_ref.at[slot], xs_ref, sems.at[slot], True, False)

    @pl.when(i == last)
    def _():
        _segment_copies(i, cnt_ref, ls_ref, gs_ref, sorted_ref.at[slot], xs_ref, sems.at[slot], True, True)


def _dispatch(cnt8, lstart, gstart, ztabs, lpos, hf, n_rows, tm):
    T, d = hf.shape
    return pl.pallas_call(
        _dispatch_kernel,
        out_shape=jax.ShapeDtypeStruct((n_rows, d), BF16),
        grid_spec=pltpu.PrefetchScalarGridSpec(
            num_scalar_prefetch=6,
            grid=(T // tm,),
            in_specs=[pl.BlockSpec((TOP_K, tm), lambda i, *_: (0, i)),
                      pl.BlockSpec((tm, d), lambda i, *_: (i, 0))],
            out_specs=pl.BlockSpec(memory_space=pl.ANY),
            scratch_shapes=[pltpu.VMEM((2, _local_rows(tm), d), BF16), pltpu.VMEM((EXP_TM, d), BF16),
                            pltpu.SemaphoreType.DMA((3,))]),
        compiler_params=_cparams(("arbitrary",)),
        name="moe_dispatch",
    )(cnt8, lstart, gstart, *ztabs, lpos, hf)


def _expert_kernel(te_ref, nu_ref, xs_ref, w1_ref, b1_ref, w2_ref, b2_ref, ys_ref, w1b_ref, w2b_ref):
    i = pl.program_id(0)
    active = i < nu_ref[0]
    new_expert = jnp.logical_or(i == 0, te_ref[i] != te_ref[jnp.maximum(i - 1, 0)])

    @pl.when(jnp.logical_and(active, new_expert))
    def _():
        w1b_ref[...] = w1_ref[0].astype(BF16)
        w2b_ref[...] = w2_ref[0].astype(BF16)

    @pl.when(active)
    def _():
        hid = _dot(xs_ref[...], w1b_ref[...]) + b1_ref[0]
        g = jnp.minimum(hid[:, :D_FF], SWIGLU_LIMIT)
        u = jnp.clip(hid[:, D_FF:], -SWIGLU_LIMIT, SWIGLU_LIMIT)
        act = g / (1.0 + jnp.exp(-SWIGLU_ALPHA * g)) * (u + 1.0)
        ys_ref[...] = (_dot(act.astype(BF16), w2b_ref[...]) + b2_ref[0]).astype(BF16)


def _experts(tile_e, n_used, xs, w1, b1, w2, b2, tm):
    n_rows, d = xs.shape
    nt = n_rows // tm
    row = lambda i, te, nu: (jnp.minimum(i, nu[0] - 1), 0)
    ex = lambda i, te, nu: (te[jnp.minimum(i, nu[0] - 1)], 0, 0)
    return pl.pallas_call(
        _expert_kernel,
        out_shape=jax.ShapeDtypeStruct((n_rows, d), BF16),
        grid_spec=pltpu.PrefetchScalarGridSpec(
            num_scalar_prefetch=2,
            grid=(nt,),
            in_specs=[pl.BlockSpec((tm, d), row),
                      pl.BlockSpec((1, d, 2 * D_FF), ex), pl.BlockSpec((1, 1, 2 * D_FF), ex),
                      pl.BlockSpec((1, D_FF, d), ex), pl.BlockSpec((1, 1, d), ex)],
            out_specs=pl.BlockSpec((tm, d), row),
            scratch_shapes=[pltpu.VMEM((d, 2 * D_FF), BF16), pltpu.VMEM((D_FF, d), BF16)]),
        compiler_params=_cparams(("arbitrary",)),
        name="moe_experts",
    )(tile_e, n_used, xs, w1, b1, w2, b2)


def _combine_kernel(cnt_ref, ls_ref, gs_ref, x1_ref, meta_ref, gf_ref, fnw_ref, ys_ref, o_ref, yl_ref, sems):
    i = pl.program_id(0)
    last = pl.num_programs(0) - 1
    slot = lax.rem(i, 2)

    def gather(tile, buf, wait):
        _segment_copies(tile, cnt_ref, ls_ref, gs_ref, yl_ref.at[buf], ys_ref, sems.at[buf], False, wait)

    @pl.when(i == 0)
    def _():
        yl_ref[...] = jnp.zeros(yl_ref.shape, BF16)
        gather(0, 0, False)

    @pl.when(i < last)
    def _():
        gather(i + 1, 1 - slot, False)

    gather(i, slot, True)

    meta = meta_ref[...]
    tm, rows = meta.shape[0], yl_ref.shape[1]
    col = lax.broadcasted_iota(I32, (tm, rows), 1).astype(F32)
    weights = jnp.zeros((tm, rows), F32)
    for k in range(TOP_K):
        weights = weights + jnp.where(col == meta[:, TOP_K + k:TOP_K + k + 1], meta[:, k:k + 1], 0.0)
    weights = weights.astype(BF16)
    x2 = x1_ref[...] + gf_ref[0] * _dot(weights, yl_ref[slot])
    o_ref[...] = _rms(x2, fnw_ref[...])


def _combine(cnt8, lstart, gstart, x1, meta, gf, fnw, ys, n_per_batch, tm):
    T, d = x1.shape
    per_b = n_per_batch // tm
    return pl.pallas_call(
        _combine_kernel,
        out_shape=jax.ShapeDtypeStruct((T, d), F32),
        grid_spec=pltpu.PrefetchScalarGridSpec(
            num_scalar_prefetch=3,
            grid=(T // tm,),
            in_specs=[pl.BlockSpec((tm, d), lambda i, *_: (i, 0)),
                      pl.BlockSpec((tm, LANES), lambda i, *_: (i, 0)),
                      pl.BlockSpec((1, 1, d), lambda i, *_: (i // per_b, 0, 0)),
                      pl.BlockSpec((1, d), lambda i, *_: (0, 0)),
                      pl.BlockSpec(memory_space=pl.ANY)],
            out_specs=pl.BlockSpec((tm, d), lambda i, *_: (i, 0)),
            scratch_shapes=[pltpu.VMEM((2, _local_rows(tm), d), BF16), pltpu.SemaphoreType.DMA((2,))]),
        compiler_params=_cparams(("arbitrary",)),
        name="moe_combine",
    )(cnt8, lstart, gstart, x1, meta, gf, fnw, ys)


def _rope_partner_perm(width):
    i = jnp.arange(width)
    blk = i // 64
    r = i % 64
    a, j, p = r // 32, (r // 16) % 2, r % 16
    return blk * 64 + a * 32 + (1 - j) * 16 + p


def _prep_weights(w_in, q_norm_w, kv_norm_w, w_uq, w_ukv):
    dq, dk, dv, cq, ckv, kr = jnp.split(w_in, [512, 1024, 1536, 1792, 1920], axis=1)
    d = w_in.shape[0]
    zeros64 = jnp.zeros((d, 64), F32)
    krp = jnp.concatenate([kr, zeros64], axis=1)
    perm512 = _rope_partner_perm(512)
    krs = jnp.concatenate([kr[:, _rope_partner_perm(64)], zeros64], axis=1)
    w_all = jnp.concatenate([dk, dv, ckv, krp, dq, cq, dk[:, perm512], krs, dq[:, perm512]], axis=1)
    wq = w_uq.reshape(MLA_Q_RANK, HEADS, MLA_NOPE + MLA_ROPE)
    z = jnp.zeros((MLA_Q_RANK, HEADS, 64), F32)
    wuq = jnp.concatenate([wq, z], axis=2).reshape(MLA_Q_RANK, HEADS * MLA_QK)
    wq_rope_sw = wq[:, :, MLA_NOPE:][:, :, _rope_partner_perm(64)]
    wuqs = jnp.concatenate([wq_rope_sw, z], axis=2).reshape(MLA_Q_RANK, HEADS * LANES)
    return {"w_all": w_all.astype(BF16), "qnw": q_norm_w[None, :], "kvnw": kv_norm_w[None, :],
            "wuq": wuq.astype(BF16), "wuqs": wuqs.astype(BF16), "wukv": w_ukv.astype(BF16)}


def _rope_tables(n):
    rows = n // GRID_W
    row = jnp.repeat(jnp.arange(rows, dtype=F32), GRID_W)
    col = jnp.tile(jnp.arange(GRID_W, dtype=F32), rows)
    inv = ROPE_THETA ** (-jnp.arange(ROPE_PAIRS, dtype=F32) / ROPE_PAIRS)
    ang = jnp.stack([row[:, None] * inv, col[:, None] * inv], axis=1)
    cos, sin = jnp.cos(ang), jnp.sin(ang)
    c64 = jnp.stack([cos, cos], axis=2).reshape(n, 64)
    s64 = jnp.stack([-sin, sin], axis=2).reshape(n, 64)
    one, zero = jnp.ones((n, 64), F32), jnp.zeros((n, 64), F32)
    return (jnp.concatenate([c64, c64], axis=1), jnp.concatenate([s64, s64], axis=1),
            jnp.concatenate([c64, one], axis=1), jnp.concatenate([s64, zero], axis=1))


def kernel(x, c, ctx, c_ctx, w_ada, b_ada, attn_norm_w, w_in, q_norm_w, kv_norm_w, w_uq, w_ukv,
           lambda_q1, lambda_k1, lambda_q2, lambda_k2, subln_w, w_out, ffn_norm_w,
           router_w, router_b, w1, b1, w2, b2, final_norm_w):
    B, n, d = x.shape
    T = B * n
    depth = w_ada.shape[0]
    assert depth == 1 and d == D_MODEL and n % PRE_TM == 0 and B + 1 <= 8
    l = 0
    lam_init = 0.8 - 0.6 * math.exp(-0.3 * l)
    lam = (jnp.exp(jnp.sum(lambda_q1[l] * lambda_k1[l])) - jnp.exp(jnp.sum(lambda_q2[l] * lambda_k2[l]))
           + lam_init).reshape(1).astype(F32)

    cc = jnp.concatenate([c, c_ctx[None, :], jnp.zeros((8 - B - 1, d), F32)], axis=0)
    mod = _ada(cc, w_ada[l], b_ada[l][None, :])
    sa, sca, ga, sf, scf, gf = [mod[:B, i * d:(i + 1) * d][:, None, :] for i in range(6)]
    csa, csca = [mod[B:B + 1, i * d:(i + 1) * d][:, None, :] for i in range(2)]

    wts = _prep_weights(w_in[l], q_norm_w[l], kv_norm_w[l], w_uq[l], w_ukv[l])
    anw = attn_norm_w[l][None, :]
    dk, dvT, mk, mvT, dqT, mqT = _pre(x, sa, sca, anw, wts, _rope_tables(n), PRE_TM)
    dkc, dvTc, mkc, mvTc = _pre(ctx, csa, csca, anw, wts, None, ctx.shape[1])

    o_diff = _attention(True, lam, lam_init, dqT, dk, dvT, dkc, dvTc, subln_w[l][:, None], ATT_TQ_DIFF)
    o_mla = _attention(False, None, lam_init, mqT, mk, mvT, mkc, mvTc, None, ATT_TQ_MLA)

    x1, hf, lpos, meta, seg = _post(
        o_diff.reshape(T, GROUP_WIDTH), o_mla.reshape(T, GROUP_WIDTH), x.reshape(T, d), ga,
        w_out[l].astype(BF16), ffn_norm_w[l][None, :], sf, scf,
        router_w[l].T, router_b[l][:, None], n, TOK_TM)

    cnt8, lstart, done = [seg[:, :, j].astype(I32) for j in range(3)]
    totals = done[-1] + cnt8[-1]
    padded = ((totals + EXP_TM - 1) // EXP_TM) * EXP_TM
    ends = jnp.cumsum(padded)
    gstart = (ends - padded)[None, :] + done
    ztabs = ((padded - totals)[None, :], jnp.zeros((1, N_EXPERTS), I32), (ends - padded + totals)[None, :])
    n_tok_tiles = T // TOK_TM
    n_tiles = -(-(T * TOP_K + n_tok_tiles * N_EXPERTS * (SEG_ALIGN - 1)) // EXP_TM) + N_EXPERTS
    tile_start = jnp.arange(n_tiles, dtype=I32) * EXP_TM
    tile_e = jnp.minimum(jnp.sum((ends[None, :] <= tile_start[:, None]).astype(I32), axis=1), N_EXPERTS - 1)
    n_used = (ends[-1:] // EXP_TM).astype(I32)

    xs = _dispatch(cnt8, lstart, gstart, ztabs, lpos, hf, n_tiles * EXP_TM, TOK_TM)
    ys = _experts(tile_e, n_used, xs, w1[l], b1[l][:, None, :], w2[l], b2[l][:, None, :], EXP_TM)
    out = _combine(cnt8, lstart, gstart, x1, meta, gf, final_norm_w[None, :], ys, n, TOK_TM)
    return out.reshape(B, n, d)
```

```python
import functools
import math

import jax
import jax.numpy as jnp
from jax import lax
from jax.experimental import pallas as pl
from jax.experimental.pallas import tpu as pltpu

F32 = jnp.float32
BF16 = jnp.bfloat16
I32 = jnp.int32

D_MODEL = 1024
GRID_W = 64
EPS = 1e-6
ROPE_THETA = 10000.0

HEADS = 4
DIFF_HEAD_DIM = 64
HEAD_V = 128
DIFF_SCALE = DIFF_HEAD_DIM ** -0.5
MLA_NOPE = 128
MLA_ROPE = 64
MLA_QK = 256
MLA_Q_RANK = 256
MLA_KV_RANK = 128
MLA_SCALE = (MLA_NOPE + MLA_ROPE) ** -0.5
LOG2E = math.log2(math.e)
ROPE_PAIRS = 16
GROUP_WIDTH = HEADS * HEAD_V

N_EXPERTS = 32
TOP_K = 4
D_FF = 1024
SWIGLU_LIMIT = 7.0
SWIGLU_ALPHA = 1.702

LANES = 128
DMA_THREADS = 2
VMEM_LIMIT = 56 * 1024 * 1024

PRE_TM = 512
ATT_TQ_DIFF = 512
ATT_TQ_MLA = 1024
ATT_CHUNKS_PER_ITER = 1
TOK_TM = 512
EXP_TM = 512
SEG_ALIGN = 16
SEG_MAX = TOK_TM
SEG_BLOCKS = tuple(SEG_ALIGN << b for b in reversed(range((SEG_MAX // SEG_ALIGN).bit_length())))


def _local_rows(tm):
    rows = TOP_K * tm + N_EXPERTS * (SEG_ALIGN - 1)
    return -(-rows // LANES) * LANES


def _cparams(sem, **kw):
    return pltpu.CompilerParams(dimension_semantics=sem, vmem_limit_bytes=VMEM_LIMIT, **kw)


def _rms(x, w):
    return x * lax.rsqrt(jnp.mean(x * x, axis=-1, keepdims=True) + EPS) * w


def _dot(a, b):
    return jnp.dot(a, b, preferred_element_type=F32)


def _dot_nt(a, b):
    return lax.dot_general(a, b, (((1,), (1,)), ((), ())), preferred_element_type=F32)


def _split_bf16(x):
    hi = x.astype(BF16)
    lo = (x - hi.astype(F32)).astype(BF16)
    return hi, lo


def _ada_kernel(c_ref, w_ref, b_ref, o_ref):
    c = c_ref[...]
    s = c / (1.0 + jnp.exp(-c))
    s_hi, s_lo = _split_bf16(s)
    w_hi, w_lo = _split_bf16(w_ref[...])
    o_ref[...] = _dot(s_hi, w_hi) + _dot(s_lo, w_hi) + _dot(s_hi, w_lo) + b_ref[...]


def _ada(cc, w, b):
    rows, d = cc.shape
    n_out = w.shape[1]
    tn = 1536
    return pl.pallas_call(
        _ada_kernel,
        out_shape=jax.ShapeDtypeStruct((rows, n_out), F32),
        grid=(n_out // tn,),
        in_specs=[pl.BlockSpec((rows, d), lambda j: (0, 0)),
                  pl.BlockSpec((d, tn), lambda j: (0, j)),
                  pl.BlockSpec((1, tn), lambda j: (0, j))],
        out_specs=pl.BlockSpec((rows, tn), lambda j: (0, j)),
        compiler_params=_cparams(("arbitrary",)),
        name="adaln",
    )(cc, w, b)


C_DK, C_DV, C_CKV, C_KR = 0, 512, 1024, 1152
C_DQ, C_CQ, C_DKS, C_KRS, C_DQS = 1280, 1792, 2048, 2560, 2688
N_COLS_CTX = 1280
N_COLS = 3200


def _pre_kernel(with_q, *refs):
    if with_q:
        (x_ref, sh_ref, sc_ref, nw_ref, w_ref, qnw_ref, kvnw_ref, wuq_ref, wuqs_ref, wukv_ref,
         cd_ref, sd_ref, ck_ref, sk_ref,
         dk_ref, dvT_ref, mk_ref, mvT_ref, dqT_ref, mqT_ref) = refs
    else:
        (x_ref, sh_ref, sc_ref, nw_ref, w_ref, kvnw_ref, wukv_ref,
         dk_ref, dvT_ref, mk_ref, mvT_ref) = refs

    x = x_ref[0]
    h = _rms(x, nw_ref[...]) * (1.0 + sc_ref[0]) + sh_ref[0]
    hb = h.astype(BF16)

    def proj(c0, width):
        return _dot(hb, w_ref[:, c0:c0 + width])

    def rope(v, vs, c, s):
        return v * c + vs * s if with_q else v

    if with_q:
        cd, sd, ck, sk = cd_ref[...], sd_ref[...], ck_ref[...], sk_ref[...]
    else:
        cd = sd = ck = sk = None

    dk = proj(C_DK, GROUP_WIDTH)
    dks = proj(C_DKS, GROUP_WIDTH) if with_q else None
    dv = proj(C_DV, GROUP_WIDTH)
    for hd in range(HEADS):
        sl = slice(hd * HEAD_V, (hd + 1) * HEAD_V)
        dk_ref[0, hd] = rope(dk[:, sl], dks[:, sl] if with_q else None, cd, sd).astype(BF16)
        dvT_ref[0, hd, 0] = dv[:, sl].T.astype(BF16)

    ckv = _rms(proj(C_CKV, MLA_KV_RANK), kvnw_ref[...]).astype(BF16)
    kv = _dot(ckv, wukv_ref[...])
    kr = rope(proj(C_KR, LANES), proj(C_KRS, LANES) if with_q else None, ck, sk).astype(BF16)
    for hd in range(HEADS):
        c0 = hd * 2 * HEAD_V
        mk_ref[0, hd, :, 0:MLA_NOPE] = kv[:, c0:c0 + MLA_NOPE].astype(BF16)
        mk_ref[0, hd, :, MLA_NOPE:MLA_QK] = kr
        mvT_ref[0, hd, 0] = kv[:, c0 + MLA_NOPE:c0 + 2 * HEAD_V].T.astype(BF16)

    if with_q:
        dq = proj(C_DQ, GROUP_WIDTH)
        dqs = proj(C_DQS, GROUP_WIDTH)
        for hd in range(HEADS):
            sl = slice(hd * HEAD_V, (hd + 1) * HEAD_V)
            dqT_ref[0, hd] = ((dq[:, sl] * cd + dqs[:, sl] * sd) * (DIFF_SCALE * LOG2E)).T.astype(BF16)
        cq = _rms(proj(C_CQ, MLA_Q_RANK), qnw_ref[...]).astype(BF16)
        q = _dot(cq, wuq_ref[...])
        qs = _dot(cq, wuqs_ref[...])
        for hd in range(HEADS):
            c0 = hd * MLA_QK
            lo = q[:, c0:c0 + MLA_NOPE]
            hi = q[:, c0 + MLA_NOPE:c0 + MLA_QK] * ck + qs[:, hd * LANES:(hd + 1) * LANES] * sk
            mqT_ref[0, hd, 0:MLA_NOPE, :] = (lo * (MLA_SCALE * LOG2E)).T.astype(BF16)
            mqT_ref[0, hd, MLA_NOPE:MLA_QK, :] = (hi * (MLA_SCALE * LOG2E)).T.astype(BF16)


def _pre(x, shift, scale, nw, wts, tables, tm):
    B, n, d = x.shape
    with_q = tables is not None
    nt = n // tm
    per_b = shift.shape[0] > 1
    mod_map = (lambda b, j: (b, 0, 0)) if per_b else (lambda b, j: (0, 0, 0))
    full = lambda shape: pl.BlockSpec(shape, lambda b, j: (0,) * len(shape))
    ncols = N_COLS if with_q else N_COLS_CTX
    w_all = wts["w_all"] if with_q else wts["w_all"][:, :ncols]

    in_specs = [pl.BlockSpec((1, tm, d), lambda b, j: (b, j, 0)),
                pl.BlockSpec((1, 1, d), mod_map), pl.BlockSpec((1, 1, d), mod_map),
                full((1, d)), full((d, ncols))]
    args = [x, shift, scale, nw, w_all]
    if with_q:
        in_specs += [full((1, MLA_Q_RANK)), full((1, MLA_KV_RANK)),
                     full((MLA_Q_RANK, HEADS * MLA_QK)), full((MLA_Q_RANK, HEADS * LANES)),
                     full((MLA_KV_RANK, HEADS * 2 * HEAD_V))]
        args += [wts["qnw"], wts["kvnw"], wts["wuq"], wts["wuqs"], wts["wukv"]]
        in_specs += [pl.BlockSpec((tm, LANES), lambda b, j: (j, 0))] * 4
        args += list(tables)
    else:
        in_specs += [full((1, MLA_KV_RANK)), full((MLA_KV_RANK, HEADS * 2 * HEAD_V))]
        args += [wts["kvnw"], wts["wukv"]]

    row_map = lambda b, j: (b, 0, j, 0)
    chunk_map = lambda b, j: (b, 0, j, 0, 0)
    out_shape = [jax.ShapeDtypeStruct((B, HEADS, n, HEAD_V), BF16),
                 jax.ShapeDtypeStruct((B, HEADS, nt, HEAD_V, tm), BF16),
                 jax.ShapeDtypeStruct((B, HEADS, n, MLA_QK), BF16),
                 jax.ShapeDtypeStruct((B, HEADS, nt, HEAD_V, tm), BF16)]
    out_specs = [pl.BlockSpec((1, HEADS, tm, HEAD_V), row_map),
                 pl.BlockSpec((1, HEADS, 1, HEAD_V, tm), chunk_map),
                 pl.BlockSpec((1, HEADS, tm, MLA_QK), row_map),
                 pl.BlockSpec((1, HEADS, 1, HEAD_V, tm), chunk_map)]
    if with_q:
        col_map = lambda b, j: (b, 0, 0, j)
        out_shape += [jax.ShapeDtypeStruct((B, HEADS, HEAD_V, n), BF16),
                      jax.ShapeDtypeStruct((B, HEADS, MLA_QK, n), BF16)]
        out_specs += [pl.BlockSpec((1, HEADS, HEAD_V, tm), col_map),
                      pl.BlockSpec((1, HEADS, MLA_QK, tm), col_map)]
    return pl.pallas_call(
        functools.partial(_pre_kernel, with_q),
        out_shape=out_shape,
        grid=(B, nt),
        in_specs=in_specs,
        out_specs=out_specs,
        compiler_params=_cparams(("arbitrary", "arbitrary")),
        name="pre_attn_latent" if with_q else "pre_attn_ctx",
    )(*args)


def _attn_kernel(diff, lam_init, n_chunks, tk, cpi, *refs):
    if diff:
        (lam_ref, qT_ref, k_ref, vT_ref, kc_ref, vTc_ref, subw_ref, o_ref, s_ref, m_ref, l_ref, acc_ref) = refs
    else:
        (qT_ref, k_ref, vT_ref, kc_ref, vTc_ref, o_ref, s_ref, m_ref, l_ref, acc_ref) = refs
    qT = qT_ref[0, 0]
    tq = qT.shape[1]
    if diff:
        row = lax.broadcasted_iota(I32, qT.shape, 0)
        zero = jnp.zeros_like(qT)
        qT = jnp.concatenate([jnp.where(row < DIFF_HEAD_DIM, qT, zero),
                              jnp.where(row >= DIFF_HEAD_DIM, qT, zero)], axis=1)

    def softmax_pv(s, vTs):
        m_prev = m_ref[...]
        m_new = jnp.maximum(m_prev, jnp.max(s, axis=0, keepdims=True))
        p = jnp.exp2(s - m_new)
        psum = jnp.sum(p, axis=0, keepdims=True)
        pb = p.astype(BF16)
        pv, r0 = None, 0
        for vT in vTs:
            part = _dot(vT, pb[r0:r0 + vT.shape[1]])
            pv = part if pv is None else pv + part
            r0 += vT.shape[1]
        alpha = jnp.exp2(m_prev - m_new)
        l_ref[...] = alpha * l_ref[...] + psum
        acc_ref[...] = alpha * acc_ref[...] + pv
        m_ref[...] = m_new

    span = cpi * tk
    n_iters = n_chunks // cpi

    def scores(j):
        return _dot(k_ref[0, 0, pl.ds(pl.multiple_of(j * span, span), span), :], qT)

    def values(j):
        return [vT_ref[0, 0, j * cpi + c] for c in range(cpi)]

    m_ref[...] = jnp.full(m_ref.shape, -jnp.inf, F32)
    l_ref[...] = jnp.zeros(l_ref.shape, F32)
    acc_ref[...] = jnp.zeros(acc_ref.shape, F32)
    s_ctx = _dot(kc_ref[0, 0], qT)
    s_first = scores(0)
    softmax_pv(s_ctx, [vTc_ref[0, 0, 0]])
    s_ref[0] = s_first

    def step(j, cur, nxt):
        s_next = scores(j + 1)
        softmax_pv(s_ref[cur], values(j))
        s_ref[nxt] = s_next

    def body(jj, carry):
        step(2 * jj, 0, 1)
        step(2 * jj + 1, 1, 0)
        return carry

    lax.fori_loop(0, (n_iters - 1) // 2, body, 0)
    last = 0
    if (n_iters - 1) % 2:
        step(n_iters - 2, 0, 1)
        last = 1
    softmax_pv(s_ref[last], values(n_iters - 1))

    o = acc_ref[...] / l_ref[...]
    if diff:
        o = o[:, :tq] - lam_ref[0] * o[:, tq:]
        o = o * lax.rsqrt(jnp.mean(o * o, axis=0, keepdims=True) + EPS) * subw_ref[...] * (1.0 - lam_init)
    o_ref[0] = o.T.astype(BF16)


def _attention(diff, lam, lam_init, qT, k, vT, kc, vTc, subw, tq):
    B, H, dk, n = qT.shape
    n_chunks, tk = vT.shape[2], vT.shape[4]
    tc = vTc.shape[4]
    cpi = ATT_CHUNKS_PER_ITER if n_chunks % ATT_CHUNKS_PER_ITER == 0 else 1
    cols = 2 * tq if diff else tq
    bh4 = lambda b, h, i: (b, h, 0, 0)
    bh5 = lambda b, h, i: (b, h, 0, 0, 0)
    in_specs = [pl.BlockSpec((1, 1, dk, tq), lambda b, h, i: (b, h, 0, i)),
                pl.BlockSpec((1, 1, n, dk), bh4),
                pl.BlockSpec((1, 1, n_chunks, HEAD_V, tk), bh5),
                pl.BlockSpec((1, 1, tc, dk), bh4),
                pl.BlockSpec((1, 1, 1, HEAD_V, tc), bh5)]
    args = [qT, k, vT, kc, vTc]
    if diff:
        in_specs = [pl.BlockSpec(memory_space=pltpu.SMEM)] + in_specs + [
            pl.BlockSpec((HEAD_V, 1), lambda b, h, i: (0, 0))]
        args = [lam] + args + [subw]
    return pl.pallas_call(
        functools.partial(_attn_kernel, diff, lam_init, n_chunks, tk, cpi),
        out_shape=jax.ShapeDtypeStruct((B, n, H * HEAD_V), BF16),
        grid=(B, H, n // tq),
        in_specs=in_specs,
        out_specs=pl.BlockSpec((1, tq, HEAD_V), lambda b, h, i: (b, i, h)),
        scratch_shapes=[pltpu.VMEM((2, cpi * tk, cols), F32), pltpu.VMEM((1, cols), F32), pltpu.VMEM((1, cols), F32),
                        pltpu.VMEM((HEAD_V, cols), F32)],
        compiler_params=_cparams(("arbitrary", "arbitrary", "arbitrary")),
        name="attn_diff" if diff else "attn_mla",
    )(*args)


def _post_kernel(od_ref, om_ref, x_ref, ga_ref, wout_ref, fw_ref, sf_ref, scf_ref, rwT_ref, rb_ref,
                 x1_ref, hf_ref, lpos_ref, meta_ref, seg_ref, carry_ref):
    i = pl.program_id(0)

    @pl.when(i == 0)
    def _():
        carry_ref[...] = jnp.zeros(carry_ref.shape, F32)

    mix = jnp.concatenate([od_ref[...], om_ref[...]], axis=1)
    x1 = x_ref[...] + ga_ref[0] * _dot(mix, wout_ref[...])
    x1_ref[...] = x1
    hf = _rms(x1, fw_ref[...]) * (1.0 + scf_ref[0]) + sf_ref[0]
    hf_ref[...] = hf.astype(BF16)

    h_hi, h_lo = _split_bf16(hf)
    w_hi, w_lo = _split_bf16(rwT_ref[...])
    logits = _dot_nt(w_hi, h_hi) + _dot_nt(w_hi, h_lo) + _dot_nt(w_lo, h_hi) + rb_ref[...]
    n_e, tm = logits.shape
    eidx = lax.broadcasted_iota(I32, (n_e, tm), 0)

    vals, idxs = [], []
    cur = logits
    for _ in range(TOP_K):
        mx = jnp.max(cur, axis=0, keepdims=True)
        idx = jnp.min(jnp.where(cur == mx, eidx, n_e), axis=0, keepdims=True)
        vals.append(mx)
        idxs.append(idx)
        cur = jnp.where(eidx == idx, -jnp.inf, cur)
    exps = [jnp.exp(v - vals[0]) for v in vals]
    denom = exps[0] + exps[1] + exps[2] + exps[3]
    gates = jnp.concatenate([e / denom for e in exps], axis=0)

    onehot = jnp.zeros((n_e, tm), F32)
    for idx in idxs:
        onehot = onehot + jnp.where(eidx == idx, 1.0, 0.0)
    r = lax.broadcasted_iota(I32, (tm, tm), 0)
    c = lax.broadcasted_iota(I32, (tm, tm), 1)
    before = jnp.where(r < c, 1.0, 0.0).astype(BF16)
    prefix = _dot(onehot.astype(BF16), before)
    cnt8 = jnp.ceil(jnp.sum(onehot, axis=1, keepdims=True) * (1.0 / SEG_ALIGN)) * SEG_ALIGN
    er = lax.broadcasted_iota(I32, (n_e, n_e), 0)
    ec = lax.broadcasted_iota(I32, (n_e, n_e), 1)
    lower = jnp.where(ec < er, 1.0, 0.0).astype(BF16)
    lstart = _dot(lower, jnp.broadcast_to(cnt8, (n_e, LANES)).astype(BF16))[:, 0:1]
    where_in_tile = prefix + lstart
    lpos = jnp.concatenate(
        [jnp.sum(jnp.where(eidx == idx, where_in_tile, 0.0), axis=0, keepdims=True) for idx in idxs], axis=0)
    done = carry_ref[...]
    carry_ref[...] = done + cnt8

    lpos_ref[...] = lpos.astype(I32)
    meta = jnp.concatenate([gates, lpos, jnp.zeros((LANES - 2 * TOP_K, tm), F32)], axis=0)
    meta_ref[...] = meta.T
    lane = lax.broadcasted_iota(I32, (n_e, LANES), 1)
    seg_ref[0] = jnp.where(lane == 0, cnt8, jnp.where(lane == 1, lstart, jnp.where(lane == 2, done, 0.0)))


def _post(od, om, x, ga, wout, fw, sf, scf, rwT, rb, n_per_batch, tm):
    T, d = x.shape
    nt = T // tm
    per_b = n_per_batch // tm
    tok = lambda w: pl.BlockSpec((tm, w), lambda i: (i, 0))
    mod = pl.BlockSpec((1, 1, d), lambda i: (i // per_b, 0, 0))
    full = lambda shape: pl.BlockSpec(shape, lambda i: (0,) * len(shape))
    return pl.pallas_call(
        _post_kernel,
        out_shape=[jax.ShapeDtypeStruct((T, d), F32), jax.ShapeDtypeStruct((T, d), BF16),
                   jax.ShapeDtypeStruct((TOP_K, T), I32), jax.ShapeDtypeStruct((T, LANES), F32),
                   jax.ShapeDtypeStruct((nt, N_EXPERTS, LANES), F32)],
        grid=(nt,),
        in_specs=[tok(GROUP_WIDTH), tok(GROUP_WIDTH), tok(d), mod, full((d, d)), full((1, d)), mod, mod,
                  full((N_EXPERTS, d)), full((N_EXPERTS, 1))],
        out_specs=[tok(d), tok(d), pl.BlockSpec((TOP_K, tm), lambda i: (0, i)), tok(LANES),
                   pl.BlockSpec((1, N_EXPERTS, LANES), lambda i: (i, 0, 0))],
        scratch_shapes=[pltpu.VMEM((N_EXPERTS, 1), F32)],
        compiler_params=_cparams(("arbitrary",)),
        name="post_attn_router",
    )(od, om, x, ga, wout, fw, sf, scf, rwT, rb)


def _segment_copies(i, cnt_ref, ls_ref, gs_ref, local_ref, hbm_ref, sem, to_hbm, wait):
    def per_expert(e, carry):
        n, lo, go = cnt_ref[i, e], ls_ref[i, e], gs_ref[i, e]
        for b, size in enumerate(SEG_BLOCKS):
            larger = (n // (2 * size)) * (2 * size)

            @pl.when((n & size) != 0)
            def _():
                loc = local_ref.at[pl.ds(pl.multiple_of(lo + larger, SEG_ALIGN), size), :]
                glob = hbm_ref.at[pl.ds(pl.multiple_of(go + larger, SEG_ALIGN), size), :]
                cp = pltpu.make_async_copy(loc, glob, sem) if to_hbm else pltpu.make_async_copy(glob, loc, sem)
                if wait:
                    cp.wait()
                else:
                    cp.start(priority=b % DMA_THREADS)
        return carry

    lax.fori_loop(0, N_EXPERTS, per_expert, 0)


def _dispatch_kernel(cnt_ref, ls_ref, gs_ref, zcnt_ref, zls_ref, zgs_ref, nu_ref, lpos_ref, hf_ref, xs_ref,
                     sorted_ref, zero_ref, sems):
    i = pl.program_id(0)
    last = pl.num_programs(0) - 1
    slot = lax.rem(i, 2)

    @pl.when(i == 0)
    def _():
        zero_ref[...] = jnp.zeros(zero_ref.shape, BF16)
        _segment_copies(0, zcnt_ref, zls_ref, zgs_ref, zero_ref, xs_ref, sems.at[2], True, False)
        _segment_copies(0, zcnt_ref, zls_ref, zgs_ref, zero_ref, xs_ref, sems.at[2], True, True)

        def unused_tile(t):
            rows = pl.ds(pl.multiple_of(t * EXP_TM, EXP_TM), EXP_TM)
            return pltpu.make_async_copy(zero_ref, xs_ref.at[rows, :], sems.at[2])

        def start(t, carry):
            unused_tile(t).start()
            return carry

        def wait(t, carry):
            unused_tile(t).wait()
            return carry

        n_tiles = xs_ref.shape[0] // EXP_TM
        lax.fori_loop(nu_ref[0], n_tiles, start, 0)
        lax.fori_loop(nu_ref[0], n_tiles, wait, 0)

    lpos = lpos_ref[...]
    rows, tm = sorted_ref.shape[1], lpos.shape[1]
    r = lax.broadcasted_iota(I32, (rows, tm), 0)
    hit = r == lpos[0:1]
    for k in range(1, TOP_K):
        hit = jnp.logical_or(hit, r == lpos[k:k + 1])
    perm = jnp.where(hit, 1.0, 0.0).astype(BF16)
    sorted_ref[slot] = _dot(perm, hf_ref[...]).astype(BF16)

    @pl.when(i > 0)
    def _():
        _segment_copies(i - 1, cnt_ref, ls_ref, gs_ref, sorted_ref.at[1 - slot], xs_ref, sems.at[1 - slot], True, True)

    _segment_copies(i, cnt_ref, ls_ref, gs_ref, sorted_ref.at[slot], xs_ref, sems.at[slot], True, False)

    @pl.when(i == last)
    def _():
        _segment_copies(i, cnt_ref, ls_ref, gs_ref, sorted_ref.at[slot], xs_ref, sems.at[slot], True, True)


def _dispatch(cnt8, lstart, gstart, ztabs, n_used, lpos, hf, n_rows, tm):
    T, d = hf.shape
    return pl.pallas_call(
        _dispatch_kernel,
        out_shape=jax.ShapeDtypeStruct((n_rows, d), BF16),
        grid_spec=pltpu.PrefetchScalarGridSpec(
            num_scalar_prefetch=7,
            grid=(T // tm,),
            in_specs=[pl.BlockSpec((TOP_K, tm), lambda i, *_: (0, i)),
                      pl.BlockSpec((tm, d), lambda i, *_: (i, 0))],
            out_specs=pl.BlockSpec(memory_space=pl.ANY),
            scratch_shapes=[pltpu.VMEM((2, _local_rows(tm), d), BF16), pltpu.VMEM((EXP_TM, d), BF16),
                            pltpu.SemaphoreType.DMA((3,))]),
        compiler_params=_cparams(("arbitrary",)),
        name="moe_dispatch",
    )(cnt8, lstart, gstart, *ztabs, n_used, lpos, hf)


def _expert_kernel(te_ref, nu_ref, xs_ref, w1_ref, b1_ref, w2_ref, b2_ref, ys_ref, w1b_ref, w2b_ref):
    i = pl.program_id(0)
    active = i < nu_ref[0]
    new_expert = jnp.logical_or(i == 0, te_ref[i] != te_ref[jnp.maximum(i - 1, 0)])

    @pl.when(jnp.logical_and(active, new_expert))
    def _():
        w1b_ref[...] = w1_ref[0].astype(BF16)
        w2b_ref[...] = w2_ref[0].astype(BF16)

    @pl.when(active)
    def _():
        hid = _dot(xs_ref[...], w1b_ref[...]) + b1_ref[0]
        g = jnp.minimum(hid[:, :D_FF], SWIGLU_LIMIT)
        u = jnp.clip(hid[:, D_FF:], -SWIGLU_LIMIT, SWIGLU_LIMIT)
        act = g / (1.0 + jnp.exp(-SWIGLU_ALPHA * g)) * (u + 1.0)
        ys_ref[...] = (_dot(act.astype(BF16), w2b_ref[...]) + b2_ref[0]).astype(BF16)

    @pl.when(jnp.logical_not(active))
    def _():
        ys_ref[...] = jnp.zeros(ys_ref.shape, BF16)


def _experts(tile_e, n_used, xs, w1, b1, w2, b2, tm):
    n_rows, d = xs.shape
    nt = n_rows // tm
    row = lambda i, te, nu: (jnp.minimum(i, nu[0] - 1), 0)
    ex = lambda i, te, nu: (te[jnp.minimum(i, nu[0] - 1)], 0, 0)
    return pl.pallas_call(
        _expert_kernel,
        out_shape=jax.ShapeDtypeStruct((n_rows, d), BF16),
        grid_spec=pltpu.PrefetchScalarGridSpec(
            num_scalar_prefetch=2,
            grid=(nt,),
            in_specs=[pl.BlockSpec((tm, d), row),
                      pl.BlockSpec((1, d, 2 * D_FF), ex), pl.BlockSpec((1, 1, 2 * D_FF), ex),
                      pl.BlockSpec((1, D_FF, d), ex), pl.BlockSpec((1, 1, d), ex)],
            out_specs=pl.BlockSpec((tm, d), lambda i, te, nu: (i, 0)),
            scratch_shapes=[pltpu.VMEM((d, 2 * D_FF), BF16), pltpu.VMEM((D_FF, d), BF16)]),
        compiler_params=_cparams(("arbitrary",)),
        name="moe_experts",
    )(tile_e, n_used, xs, w1, b1, w2, b2)


def _combine_kernel(cnt_ref, ls_ref, gs_ref, x1_ref, meta_ref, gf_ref, fnw_ref, ys_ref, o_ref, yl_ref, sems):
    i = pl.program_id(0)
    last = pl.num_programs(0) - 1
    slot = lax.rem(i, 2)

    def gather(tile, buf, wait):
        _segment_copies(tile, cnt_ref, ls_ref, gs_ref, yl_ref.at[buf], ys_ref, sems.at[buf], False, wait)

    @pl.when(i == 0)
    def _():
        yl_ref[...] = jnp.zeros(yl_ref.shape, BF16)
        gather(0, 0, False)

    @pl.when(i < last)
    def _():
        gather(i + 1, 1 - slot, False)

    gather(i, slot, True)

    meta = meta_ref[...]
    tm, rows = meta.shape[0], yl_ref.shape[1]
    col = lax.broadcasted_iota(I32, (tm, rows), 1).astype(F32)
    weights = jnp.zeros((tm, rows), F32)
    for k in range(TOP_K):
        weights = weights + jnp.where(col == meta[:, TOP_K + k:TOP_K + k + 1], meta[:, k:k + 1], 0.0)
    weights = weights.astype(BF16)
    x2 = x1_ref[...] + gf_ref[0] * _dot(weights, yl_ref[slot])
    o_ref[...] = _rms(x2, fnw_ref[...])


def _combine(cnt8, lstart, gstart, x1, meta, gf, fnw, ys, n_per_batch, tm):
    T, d = x1.shape
    per_b = n_per_batch // tm
    return pl.pallas_call(
        _combine_kernel,
        out_shape=jax.ShapeDtypeStruct((T, d), F32),
        grid_spec=pltpu.PrefetchScalarGridSpec(
            num_scalar_prefetch=3,
            grid=(T // tm,),
            in_specs=[pl.BlockSpec((tm, d), lambda i, *_: (i, 0)),
                      pl.BlockSpec((tm, LANES), lambda i, *_: (i, 0)),
                      pl.BlockSpec((1, 1, d), lambda i, *_: (i // per_b, 0, 0)),
                      pl.BlockSpec((1, d), lambda i, *_: (0, 0)),
                      pl.BlockSpec(memory_space=pl.ANY)],
            out_specs=pl.BlockSpec((tm, d), lambda i, *_: (i, 0)),
            scratch_shapes=[pltpu.VMEM((2, _local_rows(tm), d), BF16), pltpu.SemaphoreType.DMA((2,))]),
        compiler_params=_cparams(("arbitrary",)),
        name="moe_combine",
    )(cnt8, lstart, gstart, x1, meta, gf, fnw, ys)


def _rope_partner_perm(width):
    i = jnp.arange(width)
    blk = i // 64
    r = i % 64
    a, j, p = r // 32, (r // 16) % 2, r % 16
    return blk * 64 + a * 32 + (1 - j) * 16 + p


def _prep_weights(w_in, q_norm_w, kv_norm_w, w_uq, w_ukv):
    dq, dk, dv, cq, ckv, kr = jnp.split(w_in, [512, 1024, 1536, 1792, 1920], axis=1)
    d = w_in.shape[0]
    zeros64 = jnp.zeros((d, 64), F32)
    krp = jnp.concatenate([kr, zeros64], axis=1)
    perm512 = _rope_partner_perm(512)
    krs = jnp.concatenate([kr[:, _rope_partner_perm(64)], zeros64], axis=1)
    w_all = jnp.concatenate([dk, dv, ckv, krp, dq, cq, dk[:, perm512], krs, dq[:, perm512]], axis=1)
    wq = w_uq.reshape(MLA_Q_RANK, HEADS, MLA_NOPE + MLA_ROPE)
    z = jnp.zeros((MLA_Q_RANK, HEADS, 64), F32)
    wuq = jnp.concatenate([wq, z], axis=2).reshape(MLA_Q_RANK, HEADS * MLA_QK)
    wq_rope_sw = wq[:, :, MLA_NOPE:][:, :, _rope_partner_perm(64)]
    wuqs = jnp.concatenate([wq_rope_sw, z], axis=2).reshape(MLA_Q_RANK, HEADS * LANES)
    return {"w_all": w_all.astype(BF16), "qnw": q_norm_w[None, :], "kvnw": kv_norm_w[None, :],
            "wuq": wuq.astype(BF16), "wuqs": wuqs.astype(BF16), "wukv": w_ukv.astype(BF16)}


def _rope_tables(n):
    rows = n // GRID_W
    row = jnp.repeat(jnp.arange(rows, dtype=F32), GRID_W)
    col = jnp.tile(jnp.arange(GRID_W, dtype=F32), rows)
    inv = ROPE_THETA ** (-jnp.arange(ROPE_PAIRS, dtype=F32) / ROPE_PAIRS)
    ang = jnp.stack([row[:, None] * inv, col[:, None] * inv], axis=1)
    cos, sin = jnp.cos(ang), jnp.sin(ang)
    c64 = jnp.stack([cos, cos], axis=2).reshape(n, 64)
    s64 = jnp.stack([-sin, sin], axis=2).reshape(n, 64)
    one, zero = jnp.ones((n, 64), F32), jnp.zeros((n, 64), F32)
    return (jnp.concatenate([c64, c64], axis=1), jnp.concatenate([s64, s64], axis=1),
            jnp.concatenate([c64, one], axis=1), jnp.concatenate([s64, zero], axis=1))


def kernel(x, c, ctx, c_ctx, w_ada, b_ada, attn_norm_w, w_in, q_norm_w, kv_norm_w, w_uq, w_ukv,
           lambda_q1, lambda_k1, lambda_q2, lambda_k2, subln_w, w_out, ffn_norm_w,
           router_w, router_b, w1, b1, w2, b2, final_norm_w):
    B, n, d = x.shape
    T = B * n
    depth = w_ada.shape[0]
    assert depth == 1 and d == D_MODEL and n % PRE_TM == 0 and B + 1 <= 8
    l = 0
    lam_init = 0.8 - 0.6 * math.exp(-0.3 * l)
    lam = (jnp.exp(jnp.sum(lambda_q1[l] * lambda_k1[l])) - jnp.exp(jnp.sum(lambda_q2[l] * lambda_k2[l]))
           + lam_init).reshape(1).astype(F32)

    cc = jnp.concatenate([c, c_ctx[None, :], jnp.zeros((8 - B - 1, d), F32)], axis=0)
    mod = _ada(cc, w_ada[l], b_ada[l][None, :])
    sa, sca, ga, sf, scf, gf = [mod[:B, i * d:(i + 1) * d][:, None, :] for i in range(6)]
    csa, csca = [mod[B:B + 1, i * d:(i + 1) * d][:, None, :] for i in range(2)]

    wts = _prep_weights(w_in[l], q_norm_w[l], kv_norm_w[l], w_uq[l], w_ukv[l])
    anw = attn_norm_w[l][None, :]
    dk, dvT, mk, mvT, dqT, mqT = _pre(x, sa, sca, anw, wts, _rope_tables(n), PRE_TM)
    dkc, dvTc, mkc, mvTc = _pre(ctx, csa, csca, anw, wts, None, ctx.shape[1])

    o_diff = _attention(True, lam, lam_init, dqT, dk, dvT, dkc, dvTc, subln_w[l][:, None], ATT_TQ_DIFF)
    o_mla = _attention(False, None, lam_init, mqT, mk, mvT, mkc, mvTc, None, ATT_TQ_MLA)

    x1, hf, lpos, meta, seg = _post(
        o_diff.reshape(T, GROUP_WIDTH), o_mla.reshape(T, GROUP_WIDTH), x.reshape(T, d), ga,
        w_out[l].astype(BF16), ffn_norm_w[l][None, :], sf, scf,
        router_w[l].T, router_b[l][:, None], n, TOK_TM)

    cnt8, lstart, done = [seg[:, :, j].astype(I32) for j in range(3)]
    totals = done[-1] + cnt8[-1]
    padded = ((totals + EXP_TM - 1) // EXP_TM) * EXP_TM
    ends = jnp.cumsum(padded)
    gstart = (ends - padded)[None, :] + done
    ztabs = ((padded - totals)[None, :], jnp.zeros((1, N_EXPERTS), I32), (ends - padded + totals)[None, :])
    n_tok_tiles = T // TOK_TM
    n_tiles = -(-(T * TOP_K + n_tok_tiles * N_EXPERTS * (SEG_ALIGN - 1)) // EXP_TM) + N_EXPERTS
    tile_start = jnp.arange(n_tiles, dtype=I32) * EXP_TM
    tile_e = jnp.minimum(jnp.sum((ends[None, :] <= tile_start[:, None]).astype(I32), axis=1), N_EXPERTS - 1)
    n_used = (ends[-1:] // EXP_TM).astype(I32)

    xs = _dispatch(cnt8, lstart, gstart, ztabs, n_used, lpos, hf, n_tiles * EXP_TM, TOK_TM)
    ys = _experts(tile_e, n_used, xs, w1[l], b1[l][:, None, :], w2[l], b2[l][:, None, :], EXP_TM)
    out = _combine(cnt8, lstart, gstart, x1, meta, gf, final_norm_w[None, :], ys, n, TOK_TM)
    return out.reshape(B, n, d)
```
